```python
import jax, jax.numpy as jnp
from jax import lax
import numpy as np

D_MODEL = 2048
BATCH = 4
SEQ = 4096
DEPTH = 4

CHUNK = 64
HG_HEADS = 8
HG_KDIM = 128
HG_VDIM = 128
HG_FDIM = HG_HEADS * HG_KDIM
HG_WIDTH = HG_HEADS * HG_VDIM
F_MIN = 1e-6
SA_HEADS = 16
SA_KV_LATENT = 256
SA_Q_RANK = 512
SA_V_DIM = 64
SA_WIDTH = SA_HEADS * SA_V_DIM
IDX_HEADS = 16
IDX_DIM = 64
TOPK_MAX = 256
Q_BLOCK = 128
MASK_VALUE = -1e30
D_FF = ((8 * D_MODEL + 3 * 256 - 1) // (3 * 256)) * 256
ALPHA = (2 * DEPTH) ** 0.25
BETA = (8 * DEPTH) ** -0.25
ADA_SCALE = 0.1
IN_SPLITS = (HG_FDIM, HG_FDIM, HG_WIDTH, HG_WIDTH, SA_Q_RANK, SA_KV_LATENT, IDX_DIM, IDX_HEADS, D_MODEL, D_MODEL)
D_IN = sum(IN_SPLITS)
HG_I_START = 2 * HG_FDIM

kernel_name = 'hybrid_hgrn2_dsa_deepnorm_encoder'


def _split_points(sizes):
    pts, acc = [], 0
    for s in sizes[:-1]:
        acc += s
        pts.append(acc)
    return pts


def layer_norm(x, g, b, eps=1e-5):
    xf = x.astype(jnp.float32)
    mu = jnp.mean(xf, -1, keepdims=True)
    var = jnp.mean(jnp.square(xf - mu), -1, keepdims=True)
    return ((xf - mu) * lax.rsqrt(var + eps)).astype(x.dtype) * g + b


def rms_norm(x, g, eps=1e-6):
    xf = x.astype(jnp.float32)
    return (xf * lax.rsqrt(jnp.mean(xf * xf, -1, keepdims=True) + eps)).astype(x.dtype) * g


def alibi_slopes(n):
    return jnp.exp2(-8.0 * (jnp.arange(n, dtype=jnp.float32) + 1.0) / n)


def hgrn_lower_bounds(lb_logits):
    p = jax.nn.softmax(lb_logits.astype(jnp.float32), axis=0)
    return jnp.cumsum(p, axis=0) - p[0:1]


def hgrn2_chunkwise(q, k, v, log_f):
    B, L, H, K = q.shape
    V = v.shape[-1]
    nc = L // CHUNK

    def to_chunks(t):
        return t.astype(jnp.float32).reshape(B, nc, CHUNK, H, t.shape[-1]).transpose(1, 0, 3, 2, 4)

    causal = jnp.tril(jnp.ones((CHUNK, CHUNK), bool))[:, :, None]

    def step(S, inp):
        qt, kt, vt, ft = inp
        b = jnp.cumsum(ft, axis=2)
        b_last = b[:, :, -1:, :]
        o_inter = jnp.einsum('bhtk,bhkv->bhtv', qt * jnp.exp(b), S)
        diff = b[:, :, :, None, :] - b[:, :, None, :, :]
        decay = jnp.where(causal, jnp.exp(jnp.where(causal, diff, 0.0)), 0.0)
        scores = jnp.einsum('bhtk,bhtsk,bhsk->bhts', qt, decay, kt)
        o_intra = jnp.einsum('bhts,bhsv->bhtv', scores, vt)
        S_new = jnp.exp(b_last[:, :, 0, :])[..., None] * S + jnp.einsum('bhsk,bhsv->bhkv', kt * jnp.exp(b_last - b), vt)
        return S_new, o_inter + o_intra

    S0 = jnp.zeros((B, H, K, V), jnp.float32)
    _, o = lax.scan(step, S0, (to_chunks(q), to_chunks(k), to_chunks(v), to_chunks(log_f)))
    return o.transpose(1, 0, 3, 2, 4).reshape(B, L, H, V)


def dsa_attention(q, c_kv, q_idx, k_idx, idx_w, slopes):
    B, L, H, C = q.shape
    n_sel = min(TOPK_MAX, L // 4)
    nblk = L // Q_BLOCK
    pos = jnp.arange(L)
    key_chunk = pos // CHUNK
    scale = C ** -0.5

    def blocks(t):
        return jnp.moveaxis(t.reshape(B, nblk, Q_BLOCK, *t.shape[2:]), 1, 0)

    def one_block(inp):
        qb, qib, wb, pb = inp
        q_chunk = pb // CHUNK
        isc = jax.nn.relu(jnp.einsum('bqhd,bsd->bqhs', qib, k_idx))
        isc = jnp.einsum('bqhs,bqh->bqs', isc, wb).astype(jnp.float32)
        admissible = key_chunk[None, :] <= q_chunk[:, None]
        isc = jnp.where(admissible[None], isc, MASK_VALUE)
        _, sel = lax.top_k(isc, n_sel)
        kv = jax.vmap(lambda t, i: t[i])(c_kv, sel)
        valid = (sel // CHUNK) <= q_chunk[None, :, None]
        dist = jnp.abs(pb[None, :, None] - sel).astype(jnp.float32)
        s = jnp.einsum('bqhc,bqkc->bqhk', qb, kv).astype(jnp.float32) * scale
        s = s - slopes[None, None, :, None] * dist[:, :, None, :]
        s = jnp.where(valid[:, :, None, :], s, MASK_VALUE)
        p = jax.nn.softmax(s, axis=-1).astype(kv.dtype)
        return jnp.einsum('bqhk,bqkc->bqhc', p, kv)

    o = lax.map(one_block, (blocks(q), blocks(q_idx), blocks(idx_w), pos.reshape(nblk, Q_BLOCK)))
    return jnp.moveaxis(o, 0, 1).reshape(B, L, H, C)


def mixer_sublayer(h, lb, slopes, w_in, g_hg, w_pa, g_cq, g_ckv, w_uq, w_iq, w_uv, w_pb, w_out):
    B, L, _ = h.shape
    proj = h @ w_in
    q_a, f_a, i_a, g_a, c_q, c_kv, k_idx, idx_w, gate_a, gate_b = jnp.split(proj, _split_points(IN_SPLITS), axis=-1)

    heads = lambda t: t.reshape(B, L, HG_HEADS, -1)
    lb_h = lb.reshape(HG_HEADS, HG_KDIM)
    f = lb_h + (1.0 - lb_h) * jax.nn.sigmoid(heads(f_a).astype(jnp.float32))
    log_f = jnp.log(jnp.clip(f, F_MIN, 1.0))
    k_a = 1.0 - f
    o_a = hgrn2_chunkwise(heads(q_a), k_a, heads(i_a), log_f).astype(h.dtype)
    y_a = (rms_norm(o_a, g_hg.reshape(HG_HEADS, HG_VDIM)) * jax.nn.silu(heads(g_a))).reshape(B, L, HG_WIDTH)

    c_q = rms_norm(c_q, g_cq)
    q_b = (c_q @ w_uq).reshape(B, L, SA_HEADS, SA_KV_LATENT)
    c_kv = rms_norm(c_kv, g_ckv)
    q_idx = (c_q @ w_iq).reshape(B, L, IDX_HEADS, IDX_DIM)
    idx_w = idx_w * (IDX_HEADS ** -0.5 * IDX_DIM ** -0.5)
    o_b = dsa_attention(q_b, c_kv, q_idx, k_idx, idx_w, slopes)
    y_b = jnp.einsum('blhc,hcv->blhv', o_b, w_uv).reshape(B, L, SA_WIDTH)

    merged = jax.nn.sigmoid(gate_a) * (y_a @ w_pa) + jax.nn.sigmoid(gate_b) * (y_b @ w_pb)
    return merged @ w_out


def swiglu(h, w_gate, w_up, w_down):
    return (jax.nn.silu(h @ w_gate) * (h @ w_up)) @ w_down


def setup_inputs(seed: int = 0) -> dict:
    key = jax.random.key(seed)
    ks = jax.random.split(key, 22)
    nrm = lambda k, shape, s: jax.random.normal(k, shape, jnp.float32) * s
    gain = lambda k, shape: 1.0 + 0.02 * jax.random.normal(k, shape, jnp.float32)
    col_scale = jnp.ones((D_IN,), jnp.float32).at[HG_I_START:HG_I_START + HG_WIDTH].set(BETA)
    return {
        'x': nrm(ks[0], (BATCH, SEQ, D_MODEL), 1.0),
        'c': nrm(ks[1], (BATCH, D_MODEL), 1.0),
        'w_ada': nrm(ks[2], (DEPTH, D_MODEL, 6 * D_MODEL), ADA_SCALE * D_MODEL ** -0.5),
        'b_ada': nrm(ks[3], (DEPTH, 6 * D_MODEL), 0.01),
        'w_in': nrm(ks[4], (DEPTH, D_MODEL, D_IN), D_MODEL ** -0.5) * col_scale,
        'lb_logits': nrm(ks[5], (DEPTH, HG_FDIM), 0.5),
        'g_hg': gain(ks[6], (DEPTH, HG_WIDTH)),
        'w_pa': nrm(ks[7], (DEPTH, HG_WIDTH, D_MODEL), HG_WIDTH ** -0.5),
        'g_cq': gain(ks[8], (DEPTH, SA_Q_RANK)),
        'g_ckv': gain(ks[9], (DEPTH, SA_KV_LATENT)),
        'w_uq': nrm(ks[10], (DEPTH, SA_Q_RANK, SA_HEADS * SA_KV_LATENT), SA_Q_RANK ** -0.5),
        'w_iq': nrm(ks[11], (DEPTH, SA_Q_RANK, IDX_HEADS * IDX_DIM), SA_Q_RANK ** -0.5),
        'w_uv': nrm(ks[12], (DEPTH, SA_HEADS, SA_KV_LATENT, SA_V_DIM), BETA * SA_KV_LATENT ** -0.5),
        'w_pb': nrm(ks[13], (DEPTH, SA_WIDTH, D_MODEL), SA_WIDTH ** -0.5),
        'w_out': nrm(ks[14], (DEPTH, D_MODEL, D_MODEL), BETA * D_MODEL ** -0.5),
        'ln1_g': gain(ks[15], (DEPTH, D_MODEL)),
        'ln1_b': nrm(ks[16], (DEPTH, D_MODEL), 0.01),
        'w_gate': nrm(ks[17], (DEPTH, D_MODEL, D_FF), D_MODEL ** -0.5),
        'w_up': nrm(ks[18], (DEPTH, D_MODEL, D_FF), BETA * D_MODEL ** -0.5),
        'w_down': nrm(ks[19], (DEPTH, D_FF, D_MODEL), BETA * D_FF ** -0.5),
        'ln2_g': gain(ks[20], (DEPTH, D_MODEL)),
        'ln2_b': nrm(ks[21], (DEPTH, D_MODEL), 0.01),
    }


def reference(x, c, w_ada, b_ada, w_in, lb_logits, g_hg, w_pa, g_cq, g_ckv, w_uq, w_iq, w_uv, w_pb, w_out, ln1_g, ln1_b, w_gate, w_up, w_down, ln2_g, ln2_b):
    lbs = hgrn_lower_bounds(lb_logits)
    slopes = alibi_slopes(SA_HEADS)
    cond = jax.nn.silu(c)
    for l in range(DEPTH):
        mod = (cond @ w_ada[l] + b_ada[l])[:, None, :]
        sh_m, sc_m, gt_m, sh_f, sc_f, gt_f = jnp.split(mod, 6, axis=-1)
        h = x * (1.0 + sc_m) + sh_m
        y = mixer_sublayer(h, lbs[l], slopes, w_in[l], g_hg[l], w_pa[l], g_cq[l], g_ckv[l], w_uq[l], w_iq[l], w_uv[l], w_pb[l], w_out[l])
        x = layer_norm(ALPHA * x + (1.0 + gt_m) * y, ln1_g[l], ln1_b[l])
        h = x * (1.0 + sc_f) + sh_f
        y = swiglu(h, w_gate[l], w_up[l], w_down[l])
        x = layer_norm(ALPHA * x + (1.0 + gt_f) * y, ln2_g[l], ln2_b[l])
    return x
```

```python
import functools

import numpy as np
import jax
import jax.numpy as jnp
from jax import lax
from jax.experimental import pallas as pl
from jax.experimental.pallas import tpu as pltpu

F32 = jnp.float32
BF16 = jnp.bfloat16

HG_HEADS = 8
HG_DIM = 128
F_MIN = 1e-6
SA_HEADS = 16
SA_LATENT = 256
SA_Q_RANK = 512
SA_V_DIM = 64
IDX_HEADS = 16
IDX_DIM = 64
TOPK_MAX = 256
CHUNK = 64
CHUNK_SHIFT = 6
Q_BLOCK = 128
ADA_PAD_ROWS = 8

HG_WIDTH = HG_HEADS * HG_DIM
SA_WIDTH = SA_HEADS * SA_V_DIM
KI_WIDTH = 128

VMEM_LIMIT = 56 * 1024 * 1024
INT_MIN = -(2 ** 31)
MASKED_SCORE = -2e30
M_INIT = -1e30

HGRN_SUB = 256
ATT_TK = 512
IDX_TK = 256


def _dot(a, b):
    return jnp.dot(a, b, preferred_element_type=F32)


def _dot_nt(a, b):
    return lax.dot_general(a, b, (((1,), (1,)), ((), ())), preferred_element_type=F32)


def _dot_tn(a, b):
    return lax.dot_general(a, b, (((0,), (0,)), ((), ())), preferred_element_type=F32)


def _sigmoid(x):
    return 1.0 / (1.0 + jnp.exp(-x))


def _layer_norm(z, g, b):
    mu = jnp.mean(z, axis=-1, keepdims=True)
    zc = z - mu
    var = jnp.mean(zc * zc, axis=-1, keepdims=True)
    return zc * lax.rsqrt(var + 1e-5) * g + b


def _params(*sem):
    return pltpu.CompilerParams(dimension_semantics=sem, vmem_limit_bytes=VMEM_LIMIT)


def _ada_kernel(c_ref, w_ref, b_ref, o_ref):
    c = c_ref[...]
    cond = c * _sigmoid(c)
    w = w_ref[...]
    c_hi = cond.astype(BF16)
    c_lo = (cond - c_hi.astype(F32)).astype(BF16)
    w_hi = w.astype(BF16)
    w_lo = (w - w_hi.astype(F32)).astype(BF16)
    acc = _dot(c_hi, w_hi) + _dot(c_hi, w_lo) + _dot(c_lo, w_hi)
    o_ref[...] = acc + b_ref[...]


def _ada(c_pad, w_ada, b_ada):
    depth, d, n = w_ada.shape
    tn = _largest_tile(n, 1024, 128)
    return pl.pallas_call(
        _ada_kernel,
        grid=(depth, n // tn),
        in_specs=[
            pl.BlockSpec((ADA_PAD_ROWS, d), lambda l, j: (0, 0)),
            pl.BlockSpec((None, d, tn), lambda l, j: (l, 0, j)),
            pl.BlockSpec((None, 1, tn), lambda l, j: (l, 0, j)),
        ],
        out_specs=pl.BlockSpec((None, ADA_PAD_ROWS, tn), lambda l, j: (l, 0, j)),
        out_shape=jax.ShapeDtypeStruct((depth, ADA_PAD_ROWS, n), F32),
        compiler_params=_params("arbitrary", "arbitrary"),
        name="ada_mod",
    )(c_pad, w_ada, b_ada.reshape(depth, 1, n))


def _inproj_kernel(x_ref, sc_ref, sh_ref, w_ref, o_ref, h_ref):
    @pl.when(pl.program_id(1) == 0)
    def _():
        h_ref[...] = (x_ref[...] * (1.0 + sc_ref[...]) + sh_ref[...]).astype(BF16)

    o_ref[...] = _dot(h_ref[...], w_ref[...])


def _inproj(x2d, sc, sh, w, seq, tm, tn):
    t, d = x2d.shape
    n = w.shape[1]
    per_b = seq // tm
    return pl.pallas_call(
        _inproj_kernel,
        grid=(t // tm, n // tn),
        in_specs=[
            pl.BlockSpec((tm, d), lambda i, j: (i, 0)),
            pl.BlockSpec((None, 1, d), lambda i, j: (i // per_b, 0, 0)),
            pl.BlockSpec((None, 1, d), lambda i, j: (i // per_b, 0, 0)),
            pl.BlockSpec((d, tn), lambda i, j: (0, j)),
        ],
        out_specs=pl.BlockSpec((tm, tn), lambda i, j: (i, j)),
        out_shape=jax.ShapeDtypeStruct((t, n), F32),
        scratch_shapes=[pltpu.VMEM((tm, d), BF16)],
        compiler_params=_params("arbitrary", "arbitrary"),
        name="in_proj",
    )(x2d, sc, sh, w)


def _hgrn_kernel(q_ref, f_ref, v_ref, g_ref, lb_ref, gh_ref, y_ref, st_ref, *, nsub):
    sub = HGRN_SUB

    @pl.when(pl.program_id(2) == 0)
    def _():
        st_ref[...] = jnp.zeros_like(st_ref)

    lb = lb_ref[...]
    gh = gh_ref[...]
    row = lax.broadcasted_iota(jnp.int32, (sub, HG_DIM), 0)
    ti = lax.broadcasted_iota(jnp.int32, (sub, sub), 0)
    si = lax.broadcasted_iota(jnp.int32, (sub, sub), 1)
    tx = ti ^ si
    tril = jnp.where(si <= ti, 1.0, 0.0).astype(BF16)
    diag_keep = jnp.where(si <= ti, tx, 8) < 8

    def body(r, carry):
        off = pl.multiple_of(r * sub, sub)
        q = q_ref[pl.ds(off, sub), :]
        fa = f_ref[pl.ds(off, sub), :]
        v = v_ref[pl.ds(off, sub), :].astype(BF16)
        ga = g_ref[pl.ds(off, sub), :]

        f = lb + (1.0 - lb) * _sigmoid(fa)
        logf = jnp.log(jnp.clip(f, F_MIN, 1.0))
        kk = 1.0 - f

        l_hi = logf.astype(BF16)
        r1 = logf - l_hi.astype(F32)
        l_mid = r1.astype(BF16)
        l_lo = (r1 - l_mid.astype(F32)).astype(BF16)
        b = _dot(tril, l_hi) + _dot(tril, l_mid) + _dot(tril, l_lo)

        a = jnp.zeros((sub, sub), F32)
        h = sub // 2
        while h >= 8:
            n = sub // (2 * h)
            b3 = b.reshape(n, 2 * h, HG_DIM)
            e = jnp.exp(-jnp.abs(b3 - b3[:, h - 1:h, :])).reshape(sub, HG_DIM)
            second = (row & h) != 0
            qd = jnp.where(second, q * e, 0.0).astype(BF16)
            kd = jnp.where(second, 0.0, kk * e).astype(BF16)
            p = _dot_nt(qd, kd)
            a = a + (p if 2 * h == sub else jnp.where(tx < 2 * h, p, 0.0))
            h //= 2
        b3 = b.reshape(sub // 8, 8, HG_DIM)
        xd = (b3 - b3[:, 3:4, :]).reshape(sub, HG_DIM)
        qd = (q * jnp.exp(xd)).astype(BF16)
        kd = (kk * jnp.exp(-xd)).astype(BF16)
        a = a + jnp.where(diag_keep, _dot_nt(qd, kd), 0.0)

        st = st_ref[...]
        qi = (q * jnp.exp(b)).astype(BF16)
        o = _dot_nt(qi, st.astype(BF16)) + _dot(a.astype(BF16), v)
        b_last = b[sub - 1:sub, :]
        kdi = (kk * jnp.exp(b_last - b)).astype(BF16)
        st_ref[...] = st * jnp.exp(b_last) + _dot_tn(v, kdi)

        ms = jnp.mean(o * o, axis=-1, keepdims=True)
        y = o * lax.rsqrt(ms + 1e-6) * gh * (ga * _sigmoid(ga))
        y_ref[pl.ds(off, sub), :] = y.astype(BF16)
        return carry

    lax.fori_loop(0, nsub, body, 0)


def _hgrn(proj, lb, g_hg, batch, seq, lblk):
    t = proj.shape[0]
    nl = seq // lblk
    col = lambda c0: (lambda b, h, l: (b * nl + l, c0 + h))
    blk = lambda c0: pl.BlockSpec((lblk, HG_DIM), col(c0))
    vec = pl.BlockSpec((None, 1, HG_DIM), lambda b, h, l: (h, 0, 0))
    return pl.pallas_call(
        functools.partial(_hgrn_kernel, nsub=lblk // HGRN_SUB),
        grid=(batch, HG_HEADS, nl),
        in_specs=[blk(0), blk(HG_HEADS), blk(2 * HG_HEADS), blk(3 * HG_HEADS), vec, vec],
        out_specs=pl.BlockSpec((lblk, HG_DIM), col(0)),
        out_shape=jax.ShapeDtypeStruct((t, HG_WIDTH), BF16),
        scratch_shapes=[pltpu.VMEM((HG_DIM, HG_DIM), F32)],
        compiler_params=_params("arbitrary", "arbitrary", "arbitrary"),
        name="hgrn2",
    )(proj, proj, proj, proj, lb.reshape(HG_HEADS, 1, HG_DIM), g_hg.reshape(HG_HEADS, 1, HG_DIM))


def _dsa_prep_kernel(cq_ref, ckv_ref, ki_ref, gcq_ref, gckv_ref, wt_ref,
                     qt_ref, qit_ref, kidx_ref, ckvo_ref, ckvt_ref, wt_o_ref):
    cq = cq_ref[...]
    cqn = cq * lax.rsqrt(jnp.mean(cq * cq, axis=-1, keepdims=True) + 1e-6) * gcq_ref[...]
    cqn_t = cqn.T.astype(BF16)
    nq = SA_HEADS * SA_LATENT
    step = 512
    for c0 in range(0, nq, step):
        qt_ref[c0:c0 + step, :] = _dot(wt_ref[c0:c0 + step, :], cqn_t).astype(BF16)
    for c0 in range(0, IDX_HEADS * IDX_DIM, step):
        qit_ref[c0:c0 + step, :] = _dot(wt_ref[nq + c0:nq + c0 + step, :], cqn_t).astype(BF16)

    ckv = ckv_ref[...]
    ckvn = ckv * lax.rsqrt(jnp.mean(ckv * ckv, axis=-1, keepdims=True) + 1e-6) * gckv_ref[...]
    ckvo_ref[...] = ckvn.astype(BF16)
    ckvt_ref[...] = ckvn.T.astype(BF16)

    ki = ki_ref[...]
    kidx_ref[...] = ki.astype(BF16)
    ki_t = ki.T
    wt_o_ref[...] = ki_t[IDX_DIM:IDX_DIM + IDX_HEADS, :] * (IDX_HEADS ** -0.5 * IDX_DIM ** -0.5)


def _dsa_prep(proj, g_cq, g_ckv, w_qt, batch, seq, col_cq, tm):
    nq = SA_HEADS * SA_LATENT
    ni = IDX_HEADS * IDX_DIM
    per_b = seq // tm
    cq_blk = col_cq // SA_Q_RANK
    ckv_blk = (col_cq + SA_Q_RANK) // SA_LATENT
    ki_blk = (col_cq + SA_Q_RANK + SA_LATENT) // KI_WIDTH
    tok = lambda i: (i // per_b, i % per_b)
    return pl.pallas_call(
        _dsa_prep_kernel,
        grid=(batch * per_b,),
        in_specs=[
            pl.BlockSpec((tm, SA_Q_RANK), lambda i: (i, cq_blk)),
            pl.BlockSpec((tm, SA_LATENT), lambda i: (i, ckv_blk)),
            pl.BlockSpec((tm, KI_WIDTH), lambda i: (i, ki_blk)),
            pl.BlockSpec((1, SA_Q_RANK), lambda i: (0, 0)),
            pl.BlockSpec((1, SA_LATENT), lambda i: (0, 0)),
            pl.BlockSpec((nq + ni, SA_Q_RANK), lambda i: (0, 0)),
        ],
        out_specs=[
            pl.BlockSpec((None, nq, tm), lambda i: (tok(i)[0], 0, tok(i)[1])),
            pl.BlockSpec((None, ni, tm), lambda i: (tok(i)[0], 0, tok(i)[1])),
            pl.BlockSpec((None, tm, KI_WIDTH), lambda i: (tok(i)[0], tok(i)[1], 0)),
            pl.BlockSpec((None, tm, SA_LATENT), lambda i: (tok(i)[0], tok(i)[1], 0)),
            pl.BlockSpec((None, None, SA_LATENT, tm), lambda i: (tok(i)[0], tok(i)[1], 0, 0)),
            pl.BlockSpec((None, IDX_HEADS, tm), lambda i: (tok(i)[0], 0, tok(i)[1])),
        ],
        out_shape=[
            jax.ShapeDtypeStruct((batch, nq, seq), BF16),
            jax.ShapeDtypeStruct((batch, ni, seq), BF16),
            jax.ShapeDtypeStruct((batch, seq, KI_WIDTH), BF16),
            jax.ShapeDtypeStruct((batch, seq, SA_LATENT), BF16),
            jax.ShapeDtypeStruct((batch, per_b, SA_LATENT, tm), BF16),
            jax.ShapeDtypeStruct((batch, IDX_HEADS, seq), F32),
        ],
        compiler_params=_params("arbitrary"),
        name="dsa_prep",
    )(proj, proj, proj, g_cq.reshape(1, -1), g_ckv.reshape(1, -1), w_qt)


def _attn_kernel(qt_ref, qit_ref, wt_ref, kidx_ref, ckv_ref, ckvt_ref, wuvt_ref, y_ref,
                 qcat, qicat, keys, pbuf, acc, m_scr, l_scr, a_scr, ybuf, *, n_sel, slopes):
    qb = pl.program_id(1)
    lanes = Q_BLOCK
    hl = SA_HEADS * lanes

    for h in range(SA_HEADS):
        qcat[:, h * lanes:(h + 1) * lanes] = qt_ref[h * SA_LATENT:(h + 1) * SA_LATENT, :]
        qicat[0:IDX_DIM, h * lanes:(h + 1) * lanes] = qit_ref[h * IDX_DIM:(h + 1) * IDX_DIM, :]
    qicat[IDX_DIM:KI_WIDTH, :] = jnp.zeros((KI_WIDTH - IDX_DIM, hl), BF16)

    n_att = ((qb + 1) * Q_BLOCK + ATT_TK - 1) // ATT_TK
    q_chunk = jnp.right_shift(qb * Q_BLOCK + lax.broadcasted_iota(jnp.int32, (1, lanes), 1), CHUNK_SHIFT)

    def idx_body(j, carry):
        off = pl.multiple_of(j * IDX_TK, IDX_TK)
        kt = kidx_ref[pl.ds(off, IDX_TK), :]
        sc = jnp.zeros((IDX_TK, lanes), F32)
        for hp in range(IDX_HEADS // 2):
            z = _dot(kt, qicat[:, hp * 2 * lanes:(hp + 1) * 2 * lanes])
            for u in range(2):
                h = 2 * hp + u
                sc = sc + wt_ref[h:h + 1, :] * jnp.maximum(z[:, u * lanes:(u + 1) * lanes], 0.0)
        k_chunk = jnp.right_shift(off + lax.broadcasted_iota(jnp.int32, (IDX_TK, lanes), 0), CHUNK_SHIFT)
        bits = lax.bitcast_convert_type(sc, jnp.int32)
        key = jnp.where(bits < 0, bits ^ 0x7FFFFFFF, bits)
        keys[pl.ds(off, IDX_TK), :] = jnp.where(k_chunk <= q_chunk, key, INT_MIN)
        return carry

    lax.fori_loop(0, n_att * (ATT_TK // IDX_TK), idx_body, 0)

    cnt_rows = 256
    n_cnt = n_att * (ATT_TK // cnt_rows)

    def bit_body(it, thr):
        cand = thr + jnp.left_shift(jnp.int32(1), 31 - it)

        def cnt_body(c, part):
            x = keys[pl.ds(pl.multiple_of(c * cnt_rows, cnt_rows), cnt_rows), :]
            ge = jnp.where(x >= cand, 1, 0)
            return part + jnp.sum(ge.reshape(cnt_rows // 8, 8, lanes), axis=0)

        part = lax.fori_loop(0, n_cnt, cnt_body, jnp.zeros((8, lanes), jnp.int32))
        cnt = jnp.sum(part, axis=0, keepdims=True)
        return jnp.where(cnt >= n_sel, cand, thr)

    thr = lax.fori_loop(0, 32, bit_body, jnp.full((1, lanes), INT_MIN, jnp.int32))
    thr = jnp.maximum(thr, INT_MIN + 1)

    m_scr[...] = jnp.full((1, hl), M_INIT, F32)
    l_scr[...] = jnp.zeros((1, hl), F32)
    acc[...] = jnp.zeros((SA_LATENT, hl), F32)
    q_pos = qb * Q_BLOCK + lax.broadcasted_iota(jnp.int32, (ATT_TK, lanes), 1)

    def att_body(j, carry):
        off = pl.multiple_of(j * ATT_TK, ATT_TK)
        kv = ckv_ref[pl.ds(off, ATT_TK), :]
        kvt = ckvt_ref[j]
        mb = jnp.where(keys[pl.ds(off, ATT_TK), :] >= thr, 0.0, MASKED_SCORE)
        k_pos = off + lax.broadcasted_iota(jnp.int32, (ATT_TK, lanes), 0)
        dist = jnp.abs(q_pos - k_pos).astype(F32)
        for hp in range(SA_HEADS // 2):
            s2 = _dot(kv, qcat[:, hp * 2 * lanes:(hp + 1) * 2 * lanes])
            for u in range(2):
                h = 2 * hp + u
                lo = h * lanes
                s = s2[:, u * lanes:(u + 1) * lanes] - slopes[h] * dist + mb
                m_old = m_scr[:, lo:lo + lanes]
                m_new = jnp.maximum(m_old, jnp.max(s, axis=0, keepdims=True))
                alpha = jnp.exp(m_old - m_new)
                p = jnp.exp(s - m_new)
                l_scr[:, lo:lo + lanes] = alpha * l_scr[:, lo:lo + lanes] + jnp.sum(p, axis=0, keepdims=True)
                m_scr[:, lo:lo + lanes] = m_new
                a_scr[:, lo:lo + lanes] = alpha
                pbuf[:, lo:lo + lanes] = p.astype(BF16)
        for hp in range(SA_HEADS // 2):
            cs = slice(hp * 2 * lanes, (hp + 1) * 2 * lanes)
            acc[:, cs] = acc[:, cs] * a_scr[:, cs] + _dot(kvt, pbuf[:, cs])
        return carry

    lax.fori_loop(0, n_att, att_body, 0)

    for h in range(SA_HEADS):
        lo = h * lanes
        o = acc[:, lo:lo + lanes] / l_scr[:, lo:lo + lanes]
        ybuf[h * SA_V_DIM:(h + 1) * SA_V_DIM, :] = _dot(wuvt_ref[h], o.astype(BF16))
    y_ref[...] = ybuf[...].T.astype(BF16)


def _attn(qt, qit, wt, kidx, ckv, ckvt, w_uvt, batch, seq):
    nq = SA_HEADS * SA_LATENT
    ni = IDX_HEADS * IDX_DIM
    nblk = seq // Q_BLOCK
    hl = SA_HEADS * Q_BLOCK
    n_sel = min(TOPK_MAX, seq // 4)
    slopes = tuple(float(v) for v in np.exp2(-8.0 * (np.arange(SA_HEADS, dtype=np.float64) + 1.0) / SA_HEADS))
    return pl.pallas_call(
        functools.partial(_attn_kernel, n_sel=n_sel, slopes=slopes),
        grid=(batch, nblk),
        in_specs=[
            pl.BlockSpec((None, nq, Q_BLOCK), lambda b, i: (b, 0, i)),
            pl.BlockSpec((None, ni, Q_BLOCK), lambda b, i: (b, 0, i)),
            pl.BlockSpec((None, IDX_HEADS, Q_BLOCK), lambda b, i: (b, 0, i)),
            pl.BlockSpec((None, seq, KI_WIDTH), lambda b, i: (b, 0, 0)),
            pl.BlockSpec((None, seq, SA_LATENT), lambda b, i: (b, 0, 0)),
            pl.BlockSpec((None, seq // ATT_TK, SA_LATENT, ATT_TK), lambda b, i: (b, 0, 0, 0)),
            pl.BlockSpec((SA_HEADS, SA_V_DIM, SA_LATENT), lambda b, i: (0, 0, 0)),
        ],
        out_specs=pl.BlockSpec((Q_BLOCK, SA_WIDTH), lambda b, i: (b * nblk + i, 0)),
        out_shape=jax.ShapeDtypeStruct((batch * seq, SA_WIDTH), BF16),
        scratch_shapes=[
            pltpu.VMEM((SA_LATENT, hl), BF16),
            pltpu.VMEM((KI_WIDTH, hl), BF16),
            pltpu.VMEM((seq, Q_BLOCK), jnp.int32),
            pltpu.VMEM((ATT_TK, hl), BF16),
            pltpu.VMEM((SA_LATENT, hl), F32),
            pltpu.VMEM((1, hl), F32),
            pltpu.VMEM((1, hl), F32),
            pltpu.VMEM((1, hl), F32),
            pltpu.VMEM((SA_WIDTH, Q_BLOCK), F32),
        ],
        compiler_params=_params("arbitrary", "arbitrary"),
        name="dsa_attn",
    )(qt, qit, wt, kidx, ckv, ckvt, w_uvt)


def _merge_kernel(ya_ref, yb_ref, ga_ref, gb_ref, x_ref, gt_ref, wpa_ref, wpb_ref, wout_ref,
                  lng_ref, lnb_ref, o_ref, mbuf, zbuf, *, alpha, tc):
    d = x_ref.shape[1]
    ya = ya_ref[...]
    yb = yb_ref[...]
    for c0 in range(0, d, tc):
        cs = slice(c0, c0 + tc)
        m = (_sigmoid(ga_ref[:, cs]) * _dot(ya, wpa_ref[:, cs])
             + _sigmoid(gb_ref[:, cs]) * _dot(yb, wpb_ref[:, cs]))
        mbuf[:, cs] = m.astype(BF16)
    mm = mbuf[...]
    for c0 in range(0, d, tc):
        cs = slice(c0, c0 + tc)
        zbuf[:, cs] = alpha * x_ref[:, cs] + (1.0 + gt_ref[:, cs]) * _dot(mm, wout_ref[:, cs])
    o_ref[...] = _layer_norm(zbuf[...], lng_ref[...], lnb_ref[...])


def _merge(ya, yb, proj, x2d, gt, wpa, wpb, wout, lng, lnb, seq, col_ga, alpha, tm):
    t, d = x2d.shape
    per_b = seq // tm
    ga_blk = col_ga // d
    const = lambda i: (0, 0)
    return pl.pallas_call(
        functools.partial(_merge_kernel, alpha=alpha, tc=min(d, 512)),
        grid=(t // tm,),
        in_specs=[
            pl.BlockSpec((tm, HG_WIDTH), lambda i: (i, 0)),
            pl.BlockSpec((tm, SA_WIDTH), lambda i: (i, 0)),
            pl.BlockSpec((tm, d), lambda i: (i, ga_blk)),
            pl.BlockSpec((tm, d), lambda i: (i, ga_blk + 1)),
            pl.BlockSpec((tm, d), lambda i: (i, 0)),
            pl.BlockSpec((None, 1, d), lambda i: (i // per_b, 0, 0)),
            pl.BlockSpec((HG_WIDTH, d), const, pipeline_mode=pl.Buffered(1)),
            pl.BlockSpec((SA_WIDTH, d), const, pipeline_mode=pl.Buffered(1)),
            pl.BlockSpec((d, d), const, pipeline_mode=pl.Buffered(1)),
            pl.BlockSpec((1, d), const),
            pl.BlockSpec((1, d), const),
        ],
        out_specs=pl.BlockSpec((tm, d), lambda i: (i, 0)),
        out_shape=jax.ShapeDtypeStruct((t, d), F32),
        scratch_shapes=[pltpu.VMEM((tm, d), BF16), pltpu.VMEM((tm, d), F32)],
        compiler_params=_params("arbitrary"),
        name="merge_out",
    )(ya, yb, proj, proj, x2d, gt, wpa, wpb, wout, lng.reshape(1, d), lnb.reshape(1, d))


def _swiglu_kernel(x_ref, sc_ref, sh_ref, gt_ref, wg_ref, wu_ref, wd_ref, lng_ref, lnb_ref,
                   o_ref, hbuf, acc, *, alpha):
    f = pl.program_id(1)

    @pl.when(f == 0)
    def _():
        hbuf[...] = (x_ref[...] * (1.0 + sc_ref[...]) + sh_ref[...]).astype(BF16)
        acc[...] = jnp.zeros_like(acc)

    h = hbuf[...]
    g = _dot(h, wg_ref[...])
    u = _dot(h, wu_ref[...])
    a = (g * _sigmoid(g) * u).astype(BF16)
    acc[...] += _dot(a, wd_ref[...])

    @pl.when(f == pl.num_programs(1) - 1)
    def _():
        z = alpha * x_ref[...] + (1.0 + gt_ref[...]) * acc[...]
        o_ref[...] = _layer_norm(z, lng_ref[...], lnb_ref[...])


def _swiglu(x2d, sc, sh, gt, wg, wu, wd, lng, lnb, seq, alpha, tm, tf):
    t, d = x2d.shape
    dff = wg.shape[1]
    per_b = seq // tm
    mod = pl.BlockSpec((None, 1, d), lambda i, f: (i // per_b, 0, 0))
    const = pl.BlockSpec((1, d), lambda i, f: (0, 0))
    return pl.pallas_call(
        functools.partial(_swiglu_kernel, alpha=alpha),
        grid=(t // tm, dff // tf),
        in_specs=[
            pl.BlockSpec((tm, d), lambda i, f: (i, 0)),
            mod, mod, mod,
            pl.BlockSpec((d, tf), lambda i, f: (0, f)),
            pl.BlockSpec((d, tf), lambda i, f: (0, f)),
            pl.BlockSpec((tf, d), lambda i, f: (f, 0)),
            const, const,
        ],
        out_specs=pl.BlockSpec((tm, d), lambda i, f: (i, 0)),
        out_shape=jax.ShapeDtypeStruct((t, d), F32),
        scratch_shapes=[pltpu.VMEM((tm, d), BF16), pltpu.VMEM((tm, d), F32)],
        compiler_params=_params("arbitrary", "arbitrary"),
        name="swiglu",
    )(x2d, sc, sh, gt, wg, wu, wd, lng.reshape(1, d), lnb.reshape(1, d))


def _largest_tile(n, cap, quantum):
    best = quantum
    for cand in range(quantum, min(n, cap) + 1, quantum):
        if n % cand == 0:
            best = cand
    return best


def kernel(x, c, w_ada, b_ada, w_in, lb_logits, g_hg, w_pa, g_cq, g_ckv, w_uq, w_iq, w_uv, w_pb,
           w_out, ln1_g, ln1_b, w_gate, w_up, w_down, ln2_g, ln2_b):
    batch, seq, d = x.shape
    depth = w_ada.shape[0]
    dff = w_gate.shape[2]
    alpha = (2 * depth) ** 0.25
    assert seq % ATT_TK == 0 and d % 256 == 0 and batch <= ADA_PAD_ROWS

    hg4 = 4 * HG_WIDTH
    col_ga = hg4
    col_cq = hg4 + 2 * d
    used = col_cq + SA_Q_RANK + SA_LATENT + KI_WIDTH
    tn = 1024 if d >= 1024 else 512
    n_proj = -(-used // tn) * tn
    s = np.cumsum([0, HG_WIDTH, HG_WIDTH, HG_WIDTH, HG_WIDTH, SA_Q_RANK, SA_LATENT, IDX_DIM, IDX_HEADS, d, d])
    w_in_r = jnp.concatenate(
        [w_in[:, :, :s[4]], w_in[:, :, s[8]:s[10]], w_in[:, :, s[4]:s[8]],
         jnp.zeros((depth, d, n_proj - (s[10] - s[0])), w_in.dtype)], axis=2).astype(BF16)

    scale = SA_LATENT ** -0.5
    w_qt = jnp.concatenate([jnp.swapaxes(w_uq, 1, 2) * scale, jnp.swapaxes(w_iq, 1, 2)], axis=1).astype(BF16)
    w_uvt = jnp.swapaxes(w_uv, 2, 3).astype(BF16)
    w_pa_b, w_pb_b, w_out_b = w_pa.astype(BF16), w_pb.astype(BF16), w_out.astype(BF16)
    w_gate_b, w_up_b, w_down_b = w_gate.astype(BF16), w_up.astype(BF16), w_down.astype(BF16)

    p = jax.nn.softmax(lb_logits.astype(F32), axis=0)
    lbs = jnp.cumsum(p, axis=0) - p[0:1]

    c_pad = jnp.zeros((ADA_PAD_ROWS, d), F32).at[:batch].set(c)
    mods = _ada(c_pad, w_ada, b_ada)[:, :batch].reshape(depth, batch, 6, 1, d)

    tm_proj = _largest_tile(seq, 1024, 256)
    lblk = _largest_tile(seq, 1024, HGRN_SUB)
    tm_prep = ATT_TK
    tm_merge = 256
    tm_ffn = _largest_tile(seq, 512, 256)
    tf = _largest_tile(dff, 512, 128)

    x2d = x.reshape(batch * seq, d)
    for l in range(depth):
        sh_m, sc_m, gt_m, sh_f, sc_f, gt_f = (mods[l, :, k] for k in range(6))
        proj = _inproj(x2d, sc_m, sh_m, w_in_r[l], seq, tm_proj, tn)
        y_a = _hgrn(proj, lbs[l], g_hg[l], batch, seq, lblk)
        qt, qit, kidx, ckv, ckvt, wt = _dsa_prep(proj, g_cq[l], g_ckv[l], w_qt[l], batch, seq, col_cq, tm_prep)
        y_b = _attn(qt, qit, wt, kidx, ckv, ckvt, w_uvt[l], batch, seq)
        x2d = _merge(y_a, y_b, proj, x2d, gt_m, w_pa_b[l], w_pb_b[l], w_out_b[l], ln1_g[l], ln1_b[l],
                     seq, col_ga, alpha, tm_merge)
        x2d = _swiglu(x2d, sc_f, sh_f, gt_f, w_gate_b[l], w_up_b[l], w_down_b[l], ln2_g[l], ln2_b[l],
                      seq, alpha, tm_ffn, tf)
    return x2d.reshape(batch, seq, d)
```

```python
import functools

import numpy as np
import jax
import jax.numpy as jnp
from jax import lax
from jax.experimental import pallas as pl
from jax.experimental.pallas import tpu as pltpu

F32 = jnp.float32
BF16 = jnp.bfloat16

HG_HEADS = 8
HG_DIM = 128
F_MIN = 1e-6
SA_HEADS = 16
SA_LATENT = 256
SA_Q_RANK = 512
SA_V_DIM = 64
IDX_HEADS = 16
IDX_DIM = 64
TOPK_MAX = 256
CHUNK = 64
CHUNK_SHIFT = 6
Q_BLOCK = 128
ADA_PAD_ROWS = 8

HG_WIDTH = HG_HEADS * HG_DIM
SA_WIDTH = SA_HEADS * SA_V_DIM
KI_WIDTH = 128

VMEM_LIMIT = 56 * 1024 * 1024
INT_MIN = -(2 ** 31)
MASKED_SCORE = -2e30
M_INIT = -1e30

HGRN_SUB = 256
HG_GROUP = 4
ATT_TK = 512
IDX_TK = 512
CNT_ROWS = 512
KVT_ROWS = SA_LATENT + 16
LOG2E = 1.4426950408889634


def _dot(a, b):
    return jnp.dot(a, b, preferred_element_type=F32)


def _dot_nt(a, b):
    return lax.dot_general(a, b, (((1,), (1,)), ((), ())), preferred_element_type=F32)


def _dot_tn(a, b):
    return lax.dot_general(a, b, (((0,), (0,)), ((), ())), preferred_element_type=F32)


def _sigmoid(x):
    return 1.0 / (1.0 + jnp.exp(-x))


def _layer_norm(z, g, b):
    mu = jnp.mean(z, axis=-1, keepdims=True)
    zc = z - mu
    var = jnp.mean(zc * zc, axis=-1, keepdims=True)
    return zc * lax.rsqrt(var + 1e-5) * g + b


def _params(*sem):
    return pltpu.CompilerParams(dimension_semantics=sem, vmem_limit_bytes=VMEM_LIMIT)


def _ada_kernel(c_ref, w_ref, b_ref, o_ref):
    c = c_ref[...]
    cond = c * _sigmoid(c)
    w = w_ref[...]
    c_hi = cond.astype(BF16)
    c_lo = (cond - c_hi.astype(F32)).astype(BF16)
    w_hi = w.astype(BF16)
    w_lo = (w - w_hi.astype(F32)).astype(BF16)
    acc = _dot(c_hi, w_hi) + _dot(c_hi, w_lo) + _dot(c_lo, w_hi)
    o_ref[...] = acc + b_ref[...]


def _ada(c_pad, w_ada, b_ada):
    depth, d, n = w_ada.shape
    tn = _largest_tile(n, 1024, 128)
    return pl.pallas_call(
        _ada_kernel,
        grid=(depth, n // tn),
        in_specs=[
            pl.BlockSpec((ADA_PAD_ROWS, d), lambda l, j: (0, 0)),
            pl.BlockSpec((None, d, tn), lambda l, j: (l, 0, j)),
            pl.BlockSpec((None, 1, tn), lambda l, j: (l, 0, j)),
        ],
        out_specs=pl.BlockSpec((None, ADA_PAD_ROWS, tn), lambda l, j: (l, 0, j)),
        out_shape=jax.ShapeDtypeStruct((depth, ADA_PAD_ROWS, n), F32),
        compiler_params=_params("arbitrary", "arbitrary"),
        name="ada_mod",
    )(c_pad, w_ada, b_ada.reshape(depth, 1, n))


def _inproj_kernel(x_ref, sc_ref, sh_ref, w_ref, o_ref, h_ref):
    @pl.when(pl.program_id(1) == 0)
    def _():
        h_ref[...] = (x_ref[...] * (1.0 + sc_ref[...]) + sh_ref[...]).astype(BF16)

    o_ref[...] = _dot(h_ref[...], w_ref[...])


def _inproj(x2d, sc, sh, w, layer, seq, tm, tn):
    t, d = x2d.shape
    n = w.shape[2]
    per_b = seq // tm
    return pl.pallas_call(
        _inproj_kernel,
        grid=(t // tm, n // tn),
        in_specs=[
            pl.BlockSpec((tm, d), lambda i, j: (i, 0)),
            pl.BlockSpec((None, 1, d), lambda i, j: (i // per_b, 0, 0)),
            pl.BlockSpec((None, 1, d), lambda i, j: (i // per_b, 0, 0)),
            pl.BlockSpec((None, d, tn), lambda i, j: (layer, 0, j)),
        ],
        out_specs=pl.BlockSpec((tm, tn), lambda i, j: (i, j)),
        out_shape=jax.ShapeDtypeStruct((t, n), F32),
        scratch_shapes=[pltpu.VMEM((tm, d), BF16)],
        compiler_params=_params("arbitrary", "arbitrary"),
        name="in_proj",
    )(x2d, sc, sh, w)


def _hgrn_kernel(q_ref, f_ref, v_ref, g_ref, lb_ref, gh_ref, y_ref, st_ref, *, nsub):
    sub = HGRN_SUB

    @pl.when(pl.program_id(2) == 0)
    def _():
        st_ref[...] = jnp.zeros_like(st_ref)

    row = lax.broadcasted_iota(jnp.int32, (sub, HG_DIM), 0)
    ti = lax.broadcasted_iota(jnp.int32, (sub, sub), 0)
    si = lax.broadcasted_iota(jnp.int32, (sub, sub), 1)
    tx = ti ^ si
    tril = jnp.where(si <= ti, 1.0, 0.0).astype(BF16)
    diag_keep = jnp.where(si <= ti, tx, 8) < 8

    def head_block(off, hh):
        cols = slice(hh * HG_DIM, (hh + 1) * HG_DIM)
        lb = lb_ref[:, cols]
        gh = gh_ref[:, cols]
        q = q_ref[pl.ds(off, sub), cols]
        fa = f_ref[pl.ds(off, sub), cols]
        v = v_ref[pl.ds(off, sub), cols].astype(BF16)
        ga = g_ref[pl.ds(off, sub), cols]

        f = lb + (1.0 - lb) * _sigmoid(fa)
        logf = jnp.log(jnp.clip(f, F_MIN, 1.0))
        kk = 1.0 - f

        l_hi = logf.astype(BF16)
        r1 = logf - l_hi.astype(F32)
        l_mid = r1.astype(BF16)
        l_lo = (r1 - l_mid.astype(F32)).astype(BF16)
        b = _dot(tril, l_hi) + _dot(tril, l_mid) + _dot(tril, l_lo)

        a = jnp.zeros((sub, sub), F32)
        h = sub // 2
        while h >= 8:
            n = sub // (2 * h)
            b3 = b.reshape(n, 2 * h, HG_DIM)
            e = jnp.exp(-jnp.abs(b3 - b3[:, h - 1:h, :])).reshape(sub, HG_DIM)
            second = (row & h) != 0
            qd = jnp.where(second, q * e, 0.0).astype(BF16)
            kd = jnp.where(second, 0.0, kk * e).astype(BF16)
            p = _dot_nt(qd, kd)
            a = a + (p if 2 * h == sub else jnp.where(tx < 2 * h, p, 0.0))
            h //= 2
        b3 = b.reshape(sub // 8, 8, HG_DIM)
        xd = (b3 - b3[:, 3:4, :]).reshape(sub, HG_DIM)
        qd = (q * jnp.exp(xd)).astype(BF16)
        kd = (kk * jnp.exp(-xd)).astype(BF16)
        a = a + jnp.where(diag_keep, _dot_nt(qd, kd), 0.0)

        st = st_ref[hh]
        qi = (q * jnp.exp(b)).astype(BF16)
        o = _dot_nt(qi, st.astype(BF16)) + _dot(a.astype(BF16), v)
        b_last = b[sub - 1:sub, :]
        kdi = (kk * jnp.exp(b_last - b)).astype(BF16)
        st_ref[hh] = st * jnp.exp(b_last) + _dot_tn(v, kdi)

        ms = jnp.mean(o * o, axis=-1, keepdims=True)
        y = o * lax.rsqrt(ms + 1e-6) * gh * (ga * _sigmoid(ga))
        y_ref[pl.ds(off, sub), cols] = y.astype(BF16)

    def body(r, carry):
        off = pl.multiple_of(r * sub, sub)
        for hh in range(HG_GROUP):
            head_block(off, hh)
        return carry

    lax.fori_loop(0, nsub, body, 0)


def _hgrn(proj, lb, g_hg, batch, seq, lblk):
    t = proj.shape[0]
    nl = seq // lblk
    ng = HG_HEADS // HG_GROUP
    gw = HG_GROUP * HG_DIM
    col = lambda c0: (lambda b, h, l: (b * nl + l, c0 + h))
    blk = lambda c0: pl.BlockSpec((lblk, gw), col(c0))
    vec = pl.BlockSpec((None, 1, gw), lambda b, h, l: (h, 0, 0))
    return pl.pallas_call(
        functools.partial(_hgrn_kernel, nsub=lblk // HGRN_SUB),
        grid=(batch, ng, nl),
        in_specs=[blk(0), blk(ng), blk(2 * ng), blk(3 * ng), vec, vec],
        out_specs=pl.BlockSpec((lblk, gw), col(0)),
        out_shape=jax.ShapeDtypeStruct((t, HG_WIDTH), BF16),
        scratch_shapes=[pltpu.VMEM((HG_GROUP, HG_DIM, HG_DIM), F32)],
        compiler_params=_params("arbitrary", "arbitrary", "arbitrary"),
        name="hgrn2",
    )(proj, proj, proj, proj, lb.reshape(ng, 1, gw), g_hg.reshape(ng, 1, gw))


def _dsa_prep_kernel(cq_ref, ckv_ref, ki_ref, gcq_ref, gckv_ref, wt_ref,
                     qt_ref, qit_ref, kidx_ref, ckvo_ref, ckvt_ref, wt_o_ref):
    cq = cq_ref[...]
    cqn = cq * lax.rsqrt(jnp.mean(cq * cq, axis=-1, keepdims=True) + 1e-6) * gcq_ref[...]
    cqn_t = cqn.T.astype(BF16)
    nq = SA_HEADS * SA_LATENT
    step = 512
    for c0 in range(0, nq, step):
        qt_ref[c0:c0 + step, :] = _dot(wt_ref[c0:c0 + step, :], cqn_t).astype(BF16)
    for c0 in range(0, IDX_HEADS * IDX_DIM, step):
        qit_ref[c0:c0 + step, :] = _dot(wt_ref[nq + c0:nq + c0 + step, :], cqn_t).astype(BF16)

    ckv = ckv_ref[...]
    ckvn = ckv * lax.rsqrt(jnp.mean(ckv * ckv, axis=-1, keepdims=True) + 1e-6) * gckv_ref[...]
    ckvo_ref[...] = ckvn.astype(BF16)
    ckvt_ref[0:SA_LATENT, :] = ckvn.T.astype(BF16)
    ckvt_ref[SA_LATENT:KVT_ROWS, :] = jnp.ones((KVT_ROWS - SA_LATENT, ckvt_ref.shape[1]), BF16)

    ki = ki_ref[...]
    kidx_ref[...] = ki.astype(BF16)
    ki_t = ki.T
    wt_o_ref[...] = ki_t[IDX_DIM:IDX_DIM + IDX_HEADS, :] * (IDX_HEADS ** -0.5 * IDX_DIM ** -0.5)


def _dsa_prep(proj, g_cq, g_ckv, w_qt, layer, batch, seq, col_cq, tm):
    nq = SA_HEADS * SA_LATENT
    ni = IDX_HEADS * IDX_DIM
    per_b = seq // tm
    cq_blk = col_cq // SA_Q_RANK
    ckv_blk = (col_cq + SA_Q_RANK) // SA_LATENT
    ki_blk = (col_cq + SA_Q_RANK + SA_LATENT) // KI_WIDTH
    tok = lambda i: (i // per_b, i % per_b)
    return pl.pallas_call(
        _dsa_prep_kernel,
        grid=(batch * per_b,),
        in_specs=[
            pl.BlockSpec((tm, SA_Q_RANK), lambda i: (i, cq_blk)),
            pl.BlockSpec((tm, SA_LATENT), lambda i: (i, ckv_blk)),
            pl.BlockSpec((tm, KI_WIDTH), lambda i: (i, ki_blk)),
            pl.BlockSpec((1, SA_Q_RANK), lambda i: (0, 0)),
            pl.BlockSpec((1, SA_LATENT), lambda i: (0, 0)),
            pl.BlockSpec((None, nq + ni, SA_Q_RANK), lambda i: (layer, 0, 0)),
        ],
        out_specs=[
            pl.BlockSpec((None, nq, tm), lambda i: (tok(i)[0], 0, tok(i)[1])),
            pl.BlockSpec((None, ni, tm), lambda i: (tok(i)[0], 0, tok(i)[1])),
            pl.BlockSpec((None, tm, KI_WIDTH), lambda i: (tok(i)[0], tok(i)[1], 0)),
            pl.BlockSpec((None, tm, SA_LATENT), lambda i: (tok(i)[0], tok(i)[1], 0)),
            pl.BlockSpec((None, None, KVT_ROWS, tm), lambda i: (tok(i)[0], tok(i)[1], 0, 0)),
            pl.BlockSpec((None, IDX_HEADS, tm), lambda i: (tok(i)[0], 0, tok(i)[1])),
        ],
        out_shape=[
            jax.ShapeDtypeStruct((batch, nq, seq), BF16),
            jax.ShapeDtypeStruct((batch, ni, seq), BF16),
            jax.ShapeDtypeStruct((batch, seq, KI_WIDTH), BF16),
            jax.ShapeDtypeStruct((batch, seq, SA_LATENT), BF16),
            jax.ShapeDtypeStruct((batch, per_b, KVT_ROWS, tm), BF16),
            jax.ShapeDtypeStruct((batch, IDX_HEADS, seq), F32),
        ],
        compiler_params=_params("arbitrary"),
        name="dsa_prep",
    )(proj, proj, proj, g_cq.reshape(1, -1), g_ckv.reshape(1, -1), w_qt)


def _attn_kernel(qt_ref, qit_ref, wt_ref, kidx_ref, ckv_ref, ckvt_ref, wuvt_ref, y_ref,
                 qcat, qicat, keys, half, pbuf, acc, m_scr, a_scr, ybuf, *, n_sel, slopes):
    qb = pl.program_id(1)
    lanes = Q_BLOCK
    hl = SA_HEADS * lanes

    for h in range(SA_HEADS):
        qcat[:, h * lanes:(h + 1) * lanes] = qt_ref[h * SA_LATENT:(h + 1) * SA_LATENT, :]
        qicat[0:IDX_DIM, h * lanes:(h + 1) * lanes] = qit_ref[h * IDX_DIM:(h + 1) * IDX_DIM, :]
    qicat[IDX_DIM:KI_WIDTH, :] = jnp.zeros((KI_WIDTH - IDX_DIM, hl), BF16)

    n_att = ((qb + 1) * Q_BLOCK + ATT_TK - 1) // ATT_TK
    q_chunk = jnp.right_shift(qb * Q_BLOCK + lax.broadcasted_iota(jnp.int32, (1, lanes), 1), CHUNK_SHIFT)

    def idx_body(j, carry):
        off = pl.multiple_of(j * IDX_TK, IDX_TK)
        kt = kidx_ref[pl.ds(off, IDX_TK), :]
        sc = jnp.zeros((IDX_TK, lanes), F32)
        for hp in range(IDX_HEADS // 2):
            z = _dot(kt, qicat[:, hp * 2 * lanes:(hp + 1) * 2 * lanes])
            for u in range(2):
                h = 2 * hp + u
                sc = sc + wt_ref[h:h + 1, :] * jnp.maximum(z[:, u * lanes:(u + 1) * lanes], 0.0)
        k_chunk = jnp.right_shift(off + lax.broadcasted_iota(jnp.int32, (IDX_TK, lanes), 0), CHUNK_SHIFT)
        bits = lax.bitcast_convert_type(sc, jnp.int32)
        key = jnp.where(bits < 0, bits ^ 0x7FFFFFFF, bits)
        key = jnp.where(k_chunk <= q_chunk, key, INT_MIN)
        keys[pl.ds(off, IDX_TK), :] = key
        half[pl.ds(off, IDX_TK), :] = jnp.right_shift(key, 16).astype(jnp.int16)
        return carry

    lax.fori_loop(0, n_att * (ATT_TK // IDX_TK), idx_body, 0)

    cnt_rows = CNT_ROWS
    n_cnt = n_att * (ATT_TK // cnt_rows)
    i16_min = -(2 ** 15)

    def search16(need):
        def bit_body(it, t):
            cand = t + jnp.left_shift(jnp.int32(1), 15 - it)
            cand16 = jnp.broadcast_to(cand, (16, lanes)).astype(jnp.int16)

            def cnt_body(c, part):
                x = half[pl.ds(pl.multiple_of(c * cnt_rows, cnt_rows), cnt_rows), :]
                ge = [jnp.where(x[r0:r0 + 16, :] >= cand16, jnp.int16(1), jnp.int16(0))
                      for r0 in range(0, cnt_rows, 16)]
                while len(ge) > 1:
                    ge = [a + b for a, b in zip(ge[0::2], ge[1::2])]
                return part + ge[0]

            part = lax.fori_loop(0, n_cnt, cnt_body, jnp.zeros((16, lanes), jnp.int16))
            cnt = jnp.sum(part.astype(jnp.int32), axis=0, keepdims=True)
            return jnp.where(cnt >= need, cand, t)

        return lax.fori_loop(0, 16, bit_body, jnp.full((1, lanes), i16_min, jnp.int32))

    t_hi = search16(jnp.full((1, lanes), n_sel, jnp.int32))

    def low_body(c, above):
        rows = pl.ds(pl.multiple_of(c * cnt_rows, cnt_rows), cnt_rows)
        k = keys[rows, :]
        hi = jnp.right_shift(k, 16)
        lo = (k & 0xFFFF) + i16_min
        half[rows, :] = jnp.where(hi == t_hi, lo, i16_min).astype(jnp.int16)
        gt = jnp.where(hi > t_hi, 1, 0)
        return above + jnp.sum(gt.reshape(cnt_rows // 8, 8, lanes), axis=0)

    above = lax.fori_loop(0, n_cnt, low_body, jnp.zeros((8, lanes), jnp.int32))
    t_lo = search16(n_sel - jnp.sum(above, axis=0, keepdims=True))
    thr = jnp.left_shift(t_hi, 16) + (t_lo - i16_min)
    thr = jnp.maximum(thr, INT_MIN + 1)

    m_scr[...] = jnp.full((1, hl), M_INIT, F32)
    acc[...] = jnp.zeros((KVT_ROWS, hl), F32)
    q_pos = qb * Q_BLOCK + lax.broadcasted_iota(jnp.int32, (ATT_TK, lanes), 1)

    def softmax_tile(j):
        off = pl.multiple_of(j * ATT_TK, ATT_TK)
        kv = ckv_ref[pl.ds(off, ATT_TK), :]
        mb = jnp.where(keys[pl.ds(off, ATT_TK), :] >= thr, 0.0, MASKED_SCORE)
        k_pos = off + lax.broadcasted_iota(jnp.int32, (ATT_TK, lanes), 0)
        dist = jnp.abs(q_pos - k_pos).astype(F32)
        for hp in range(SA_HEADS // 2):
            s2 = _dot(kv, qcat[:, hp * 2 * lanes:(hp + 1) * 2 * lanes])
            for u in range(2):
                h = 2 * hp + u
                lo = h * lanes
                s = s2[:, u * lanes:(u + 1) * lanes] + (mb - slopes[h] * dist)
                m_old = m_scr[:, lo:lo + lanes]
                m_new = jnp.maximum(m_old, jnp.max(s, axis=0, keepdims=True))
                a_scr[:, lo:lo + lanes] = jnp.exp2(m_old - m_new)
                m_scr[:, lo:lo + lanes] = m_new
                pbuf[:, lo:lo + lanes] = jnp.exp2(s - m_new).astype(BF16)

    def value_tile(j):
        kvt = ckvt_ref[j]
        for hp in range(SA_HEADS // 2):
            cs = slice(hp * 2 * lanes, (hp + 1) * 2 * lanes)
            acc[:, cs] = acc[:, cs] * a_scr[:, cs] + _dot(kvt, pbuf[:, cs])

    def att_body(j, carry):
        softmax_tile(j)
        value_tile(j)
        return carry

    lax.fori_loop(0, n_att, att_body, 0)

    for h in range(SA_HEADS):
        lo = h * lanes
        o = acc[0:SA_LATENT, lo:lo + lanes] / acc[SA_LATENT:SA_LATENT + 1, lo:lo + lanes]
        ybuf[h * SA_V_DIM:(h + 1) * SA_V_DIM, :] = _dot(wuvt_ref[h], o.astype(BF16))
    y_ref[...] = ybuf[...].T.astype(BF16)


def _attn(qt, qit, wt, kidx, ckv, ckvt, w_uvt, layer, batch, seq):
    nq = SA_HEADS * SA_LATENT
    ni = IDX_HEADS * IDX_DIM
    nblk = seq // Q_BLOCK
    hl = SA_HEADS * Q_BLOCK
    n_sel = min(TOPK_MAX, seq // 4)
    slopes = tuple(float(v) * LOG2E
                   for v in np.exp2(-8.0 * (np.arange(SA_HEADS, dtype=np.float64) + 1.0) / SA_HEADS))
    return pl.pallas_call(
        functools.partial(_attn_kernel, n_sel=n_sel, slopes=slopes),
        grid=(batch, nblk),
        in_specs=[
            pl.BlockSpec((None, nq, Q_BLOCK), lambda b, i: (b, 0, i)),
            pl.BlockSpec((None, ni, Q_BLOCK), lambda b, i: (b, 0, i)),
            pl.BlockSpec((None, IDX_HEADS, Q_BLOCK), lambda b, i: (b, 0, i)),
            pl.BlockSpec((None, seq, KI_WIDTH), lambda b, i: (b, 0, 0)),
            pl.BlockSpec((None, seq, SA_LATENT), lambda b, i: (b, 0, 0)),
            pl.BlockSpec((None, seq // ATT_TK, KVT_ROWS, ATT_TK), lambda b, i: (b, 0, 0, 0)),
            pl.BlockSpec((None, SA_HEADS, SA_V_DIM, SA_LATENT), lambda b, i: (layer, 0, 0, 0)),
        ],
        out_specs=pl.BlockSpec((Q_BLOCK, SA_WIDTH), lambda b, i: (b * nblk + i, 0)),
        out_shape=jax.ShapeDtypeStruct((batch * seq, SA_WIDTH), BF16),
        scratch_shapes=[
            pltpu.VMEM((SA_LATENT, hl), BF16),
            pltpu.VMEM((KI_WIDTH, hl), BF16),
            pltpu.VMEM((seq, Q_BLOCK), jnp.int32),
            pltpu.VMEM((seq, Q_BLOCK), jnp.int16),
            pltpu.VMEM((ATT_TK, hl), BF16),
            pltpu.VMEM((KVT_ROWS, hl), F32),
            pltpu.VMEM((1, hl), F32),
            pltpu.VMEM((1, hl), F32),
            pltpu.VMEM((SA_WIDTH, Q_BLOCK), F32),
        ],
        compiler_params=_params("arbitrary", "arbitrary"),
        name="dsa_attn",
    )(qt, qit, wt, kidx, ckv, ckvt, w_uvt)


def _merge_kernel(ya_ref, yb_ref, ga_ref, gb_ref, x_ref, gt_ref, wpa_ref, wpb_ref, wout_ref,
                  lng_ref, lnb_ref, o_ref, mbuf, zbuf, *, alpha, tc):
    d = x_ref.shape[1]
    ya = ya_ref[...]
    yb = yb_ref[...]
    for c0 in range(0, d, tc):
        cs = slice(c0, c0 + tc)
        m = (_sigmoid(ga_ref[:, cs]) * _dot(ya, wpa_ref[:, cs])
             + _sigmoid(gb_ref[:, cs]) * _dot(yb, wpb_ref[:, cs]))
        mbuf[:, cs] = m.astype(BF16)
    mm = mbuf[...]
    for c0 in range(0, d, tc):
        cs = slice(c0, c0 + tc)
        zbuf[:, cs] = alpha * x_ref[:, cs] + (1.0 + gt_ref[:, cs]) * _dot(mm, wout_ref[:, cs])
    o_ref[...] = _layer_norm(zbuf[...], lng_ref[...], lnb_ref[...])


def _merge(ya, yb, proj, x2d, gt, wpa, wpb, wout, lng, lnb, layer, seq, col_ga, alpha, tm):
    t, d = x2d.shape
    per_b = seq // tm
    ga_blk = col_ga // d
    const = lambda i: (0, 0)
    wsel = lambda i: (layer, 0, 0)
    return pl.pallas_call(
        functools.partial(_merge_kernel, alpha=alpha, tc=min(d, 512)),
        grid=(t // tm,),
        in_specs=[
            pl.BlockSpec((tm, HG_WIDTH), lambda i: (i, 0)),
            pl.BlockSpec((tm, SA_WIDTH), lambda i: (i, 0)),
            pl.BlockSpec((tm, d), lambda i: (i, ga_blk)),
            pl.BlockSpec((tm, d), lambda i: (i, ga_blk + 1)),
            pl.BlockSpec((tm, d), lambda i: (i, 0)),
            pl.BlockSpec((None, 1, d), lambda i: (i // per_b, 0, 0)),
            pl.BlockSpec((None, HG_WIDTH, d), wsel, pipeline_mode=pl.Buffered(1)),
            pl.BlockSpec((None, SA_WIDTH, d), wsel, pipeline_mode=pl.Buffered(1)),
            pl.BlockSpec((None, d, d), wsel, pipeline_mode=pl.Buffered(1)),
            pl.BlockSpec((1, d), const),
            pl.BlockSpec((1, d), const),
        ],
        out_specs=pl.BlockSpec((tm, d), lambda i: (i, 0)),
        out_shape=jax.ShapeDtypeStruct((t, d), F32),
        scratch_shapes=[pltpu.VMEM((tm, d), BF16), pltpu.VMEM((tm, d), F32)],
        compiler_params=_params("arbitrary"),
        name="merge_out",
    )(ya, yb, proj, proj, x2d, gt, wpa, wpb, wout, lng.reshape(1, d), lnb.reshape(1, d))


def _swiglu_kernel(x_ref, sc_ref, sh_ref, gt_ref, wg_ref, wu_ref, wd_ref, lng_ref, lnb_ref,
                   o_ref, hbuf, acc, *, alpha):
    f = pl.program_id(1)

    @pl.when(f == 0)
    def _():
        hbuf[...] = (x_ref[...] * (1.0 + sc_ref[...]) + sh_ref[...]).astype(BF16)
        acc[...] = jnp.zeros_like(acc)

    h = hbuf[...]
    g = _dot(h, wg_ref[...])
    u = _dot(h, wu_ref[...])
    a = (g * _sigmoid(g) * u).astype(BF16)
    acc[...] += _dot(a, wd_ref[...])

    @pl.when(f == pl.num_programs(1) - 1)
    def _():
        z = alpha * x_ref[...] + (1.0 + gt_ref[...]) * acc[...]
        o_ref[...] = _layer_norm(z, lng_ref[...], lnb_ref[...])


def _swiglu(x2d, sc, sh, gt, wg, wu, wd, lng, lnb, layer, seq, alpha, tm, tf):
    t, d = x2d.shape
    dff = wg.shape[2]
    per_b = seq // tm
    mod = pl.BlockSpec((None, 1, d), lambda i, f: (i // per_b, 0, 0))
    const = pl.BlockSpec((1, d), lambda i, f: (0, 0))
    return pl.pallas_call(
        functools.partial(_swiglu_kernel, alpha=alpha),
        grid=(t // tm, dff // tf),
        in_specs=[
            pl.BlockSpec((tm, d), lambda i, f: (i, 0)),
            mod, mod, mod,
            pl.BlockSpec((None, d, tf), lambda i, f: (layer, 0, f)),
            pl.BlockSpec((None, d, tf), lambda i, f: (layer, 0, f)),
            pl.BlockSpec((None, tf, d), lambda i, f: (layer, f, 0)),
            const, const,
        ],
        out_specs=pl.BlockSpec((tm, d), lambda i, f: (i, 0)),
        out_shape=jax.ShapeDtypeStruct((t, d), F32),
        scratch_shapes=[pltpu.VMEM((tm, d), BF16), pltpu.VMEM((tm, d), F32)],
        compiler_params=_params("arbitrary", "arbitrary"),
        name="swiglu",
    )(x2d, sc, sh, gt, wg, wu, wd, lng.reshape(1, d), lnb.reshape(1, d))


def _largest_tile(n, cap, quantum):
    best = quantum
    for cand in range(quantum, min(n, cap) + 1, quantum):
        if n % cand == 0:
            best = cand
    return best


def kernel(x, c, w_ada, b_ada, w_in, lb_logits, g_hg, w_pa, g_cq, g_ckv, w_uq, w_iq, w_uv, w_pb,
           w_out, ln1_g, ln1_b, w_gate, w_up, w_down, ln2_g, ln2_b):
    batch, seq, d = x.shape
    depth = w_ada.shape[0]
    dff = w_gate.shape[2]
    alpha = (2 * depth) ** 0.25
    assert seq % ATT_TK == 0 and d % 256 == 0 and batch <= ADA_PAD_ROWS

    hg4 = 4 * HG_WIDTH
    col_ga = hg4
    col_cq = hg4 + 2 * d
    used = col_cq + SA_Q_RANK + SA_LATENT + KI_WIDTH
    tn = 1024 if d >= 1024 else 512
    n_proj = -(-used // tn) * tn
    s = np.cumsum([0, HG_WIDTH, HG_WIDTH, HG_WIDTH, HG_WIDTH, SA_Q_RANK, SA_LATENT, IDX_DIM, IDX_HEADS, d, d])
    w_in_r = jnp.concatenate(
        [w_in[:, :, :s[4]], w_in[:, :, s[8]:s[10]], w_in[:, :, s[4]:s[8]],
         jnp.zeros((depth, d, n_proj - (s[10] - s[0])), w_in.dtype)], axis=2).astype(BF16)

    scale = SA_LATENT ** -0.5 * LOG2E
    w_qt =jnp.concatenate([jnp.swapaxes(w_uq, 1, 2) * scale, jnp.swapaxes(w_iq, 1, 2)], axis=1).astype(BF16)
    w_uvt = jnp.swapaxes(w_uv, 2, 3).astype(BF16)
    w_pa_b, w_pb_b, w_out_b = w_pa.astype(BF16), w_pb.astype(BF16), w_out.astype(BF16)
    w_gate_b, w_up_b, w_down_b = w_gate.astype(BF16), w_up.astype(BF16), w_down.astype(BF16)

    p = jax.nn.softmax(lb_logits.astype(F32), axis=0)
    lbs = jnp.cumsum(p, axis=0) - p[0:1]

    c_pad = jnp.zeros((ADA_PAD_ROWS, d), F32).at[:batch].set(c)
    mods = _ada(c_pad, w_ada, b_ada)[:, :batch].reshape(depth, batch, 6, 1, d)

    tm_proj = _largest_tile(seq, 1024, 256)
    lblk = _largest_tile(seq, 1024, HGRN_SUB)
    tm_prep = ATT_TK
    tm_merge = 256
    tm_ffn = _largest_tile(seq, 512, 256)
    tf = _largest_tile(dff, 512, 128)

    x2d = x.reshape(batch * seq, d)
    for l in range(depth):
        sh_m, sc_m, gt_m, sh_f, sc_f, gt_f = (mods[l, :, k] for k in range(6))
        proj = _inproj(x2d, sc_m, sh_m, w_in_r, l, seq, tm_proj, tn)
        y_a = _hgrn(proj, lbs[l], g_hg[l], batch, seq, lblk)
        qt, qit, kidx, ckv, ckvt, wt = _dsa_prep(proj, g_cq[l], g_ckv[l], w_qt, l, batch, seq, col_cq, tm_prep)
        y_b = _attn(qt, qit, wt, kidx, ckv, ckvt, w_uvt, l, batch, seq)
        x2d = _merge(y_a, y_b, proj, x2d, gt_m, w_pa_b, w_pb_b, w_out_b, ln1_g[l], ln1_b[l],
                     l, seq, col_ga, alpha, tm_merge)
        x2d = _swiglu(x2d, sc_f, sh_f, gt_f, w_gate_b, w_up_b, w_down_b, ln2_g[l], ln2_b[l],
                      l, seq, alpha, tm_ffn, tf)
    return x2d.reshape(batch, seq, d)
```

```python
import functools

import numpy as np
import jax
import jax.numpy as jnp
from jax import lax
from jax.experimental import pallas as pl
from jax.experimental.pallas import tpu as pltpu

F32 = jnp.float32
BF16 = jnp.bfloat16

HG_HEADS = 8
HG_DIM = 128
F_MIN = 1e-6
SA_HEADS = 16
SA_LATENT = 256
SA_Q_RANK = 512
SA_V_DIM = 64
IDX_HEADS = 16
IDX_DIM = 64
TOPK_MAX = 256
CHUNK = 64
CHUNK_SHIFT = 6
Q_BLOCK = 128
ADA_PAD_ROWS = 8

HG_WIDTH = HG_HEADS * HG_DIM
SA_WIDTH = SA_HEADS * SA_V_DIM
KI_WIDTH = 128

VMEM_LIMIT = 56 * 1024 * 1024
INT_MIN = -(2 ** 31)
MASKED_DIST = 1e32
M_INIT = -1e29

HGRN_SUB = 256
HG_GROUP = 4
ATT_TK = 512
IDX_TK = 512
COUNT_CHAINS = 8
KVT_ROWS = SA_LATENT + 16
LOG2E = 1.4426950408889634


def _dot(a, b):
    return jnp.dot(a, b, preferred_element_type=F32)


def _dot_nt(a, b):
    return lax.dot_general(a, b, (((1,), (1,)), ((), ())), preferred_element_type=F32)


def _dot_tn(a, b):
    return lax.dot_general(a, b, (((0,), (0,)), ((), ())), preferred_element_type=F32)


def _sigmoid(x):
    return 1.0 / (1.0 + jnp.exp(-x))


def _layer_norm(z, g, b):
    mu = jnp.mean(z, axis=-1, keepdims=True)
    zc = z - mu
    var = jnp.mean(zc * zc, axis=-1, keepdims=True)
    return zc * lax.rsqrt(var + 1e-5) * g + b


def _params(*sem):
    return pltpu.CompilerParams(dimension_semantics=sem, vmem_limit_bytes=VMEM_LIMIT)


def _ada_kernel(c_ref, w_ref, b_ref, o_ref):
    c = c_ref[...]
    cond = c * _sigmoid(c)
    w = w_ref[...]
    c_hi = cond.astype(BF16)
    c_lo = (cond - c_hi.astype(F32)).astype(BF16)
    w_hi = w.astype(BF16)
    w_lo = (w - w_hi.astype(F32)).astype(BF16)
    acc = _dot(c_hi, w_hi) + _dot(c_hi, w_lo) + _dot(c_lo, w_hi)
    o_ref[...] = acc + b_ref[...]


def _ada(c_pad, w_ada, b_ada):
    depth, d, n = w_ada.shape
    tn = _largest_tile(n, 1024, 128)
    return pl.pallas_call(
        _ada_kernel,
        grid=(depth, n // tn),
        in_specs=[
            pl.BlockSpec((ADA_PAD_ROWS, d), lambda l, j: (0, 0)),
            pl.BlockSpec((None, d, tn), lambda l, j: (l, 0, j)),
            pl.BlockSpec((None, 1, tn), lambda l, j: (l, 0, j)),
        ],
        out_specs=pl.BlockSpec((None, ADA_PAD_ROWS, tn), lambda l, j: (l, 0, j)),
        out_shape=jax.ShapeDtypeStruct((depth, ADA_PAD_ROWS, n), F32),
        compiler_params=_params("arbitrary", "arbitrary"),
        name="ada_mod",
    )(c_pad, w_ada, b_ada.reshape(depth, 1, n))


def _inproj_kernel(x_ref, sc_ref, sh_ref, w_ref, o_ref, h_ref):
    @pl.when(pl.program_id(1) == 0)
    def _():
        h_ref[...] = (x_ref[...] * (1.0 + sc_ref[...]) + sh_ref[...]).astype(BF16)

    o_ref[...] = _dot(h_ref[...], w_ref[...])


def _inproj(x2d, sc, sh, w, layer, seq, tm, tn):
    t, d = x2d.shape
    n = w.shape[2]
    per_b = seq // tm
    return pl.pallas_call(
        _inproj_kernel,
        grid=(t // tm, n // tn),
        in_specs=[
            pl.BlockSpec((tm, d), lambda i, j: (i, 0)),
            pl.BlockSpec((None, 1, d), lambda i, j: (i // per_b, 0, 0)),
            pl.BlockSpec((None, 1, d), lambda i, j: (i // per_b, 0, 0)),
            pl.BlockSpec((None, d, tn), lambda i, j: (layer, 0, j)),
        ],
        out_specs=pl.BlockSpec((tm, tn), lambda i, j: (i, j)),
        out_shape=jax.ShapeDtypeStruct((t, n), F32),
        scratch_shapes=[pltpu.VMEM((tm, d), BF16)],
        compiler_params=_params("arbitrary", "arbitrary"),
        name="in_proj",
    )(x2d, sc, sh, w)


def _hgrn_kernel(q_ref, f_ref, v_ref, g_ref, lb_ref, gh_ref, y_ref, st_ref, *, nsub):
    sub = HGRN_SUB

    @pl.when(pl.program_id(2) == 0)
    def _():
        st_ref[...] = jnp.zeros_like(st_ref)

    row = lax.broadcasted_iota(jnp.int32, (sub, HG_DIM), 0)
    ti = lax.broadcasted_iota(jnp.int32, (sub, sub), 0)
    si = lax.broadcasted_iota(jnp.int32, (sub, sub), 1)
    tx = ti ^ si
    tril = jnp.where(si <= ti, 1.0, 0.0).astype(BF16)
    diag_keep = jnp.where(si <= ti, tx, 8) < 8

    def head_block(off, hh):
        cols = slice(hh * HG_DIM, (hh + 1) * HG_DIM)
        lb = lb_ref[:, cols]
        gh = gh_ref[:, cols]
        q = q_ref[pl.ds(off, sub), cols]
        fa = f_ref[pl.ds(off, sub), cols]
        v = v_ref[pl.ds(off, sub), cols].astype(BF16)
        ga = g_ref[pl.ds(off, sub), cols]

        f = lb + (1.0 - lb) * _sigmoid(fa)
        logf = jnp.log(jnp.clip(f, F_MIN, 1.0))
        kk = 1.0 - f

        l_hi = logf.astype(BF16)
        r1 = logf - l_hi.astype(F32)
        l_mid = r1.astype(BF16)
        l_lo = (r1 - l_mid.astype(F32)).astype(BF16)
        b = _dot(tril, l_hi) + _dot(tril, l_mid) + _dot(tril, l_lo)

        a = jnp.zeros((sub, sub), F32)
        h = sub // 2
        while h >= 8:
            n = sub // (2 * h)
            b3 = b.reshape(n, 2 * h, HG_DIM)
            e = jnp.exp(-jnp.abs(b3 - b3[:, h - 1:h, :])).reshape(sub, HG_DIM)
            second = (row & h) != 0
            qd = jnp.where(second, q * e, 0.0).astype(BF16)
            kd = jnp.where(second, 0.0, kk * e).astype(BF16)
            p = _dot_nt(qd, kd)
            a = a + (p if 2 * h == sub else jnp.where(tx < 2 * h, p, 0.0))
            h //= 2
        b3 = b.reshape(sub // 8, 8, HG_DIM)
        xd = (b3 - b3[:, 3:4, :]).reshape(sub, HG_DIM)
        qd = (q * jnp.exp(xd)).astype(BF16)
        kd = (kk * jnp.exp(-xd)).astype(BF16)
        a = a + jnp.where(diag_keep, _dot_nt(qd, kd), 0.0)

        st = st_ref[hh]
        qi = (q * jnp.exp(b)).astype(BF16)
        o = _dot_nt(qi, st.astype(BF16)) + _dot(a.astype(BF16), v)
        b_last = b[sub - 1:sub, :]
        kdi = (kk * jnp.exp(b_last - b)).astype(BF16)
        st_ref[hh] = st * jnp.exp(b_last) + _dot_tn(v, kdi)

        ms = jnp.mean(o * o, axis=-1, keepdims=True)
        y = o * lax.rsqrt(ms + 1e-6) * gh * (ga * _sigmoid(ga))
        y_ref[pl.ds(off, sub), cols] = y.astype(BF16)

    def body(r, carry):
        off = pl.multiple_of(r * sub, sub)
        for hh in range(HG_GROUP):
            head_block(off, hh)
        return carry

    lax.fori_loop(0, nsub, body, 0)


def _hgrn(proj, lb, g_hg, batch, seq, lblk):
    t = proj.shape[0]
    nl = seq // lblk
    ng = HG_HEADS // HG_GROUP
    gw = HG_GROUP * HG_DIM
    col = lambda c0: (lambda b, h, l: (b * nl + l, c0 + h))
    blk = lambda c0: pl.BlockSpec((lblk, gw), col(c0))
    vec = pl.BlockSpec((None, 1, gw), lambda b, h, l: (h, 0, 0))
    return pl.pallas_call(
        functools.partial(_hgrn_kernel, nsub=lblk // HGRN_SUB),
        grid=(batch, ng, nl),
        in_specs=[blk(0), blk(ng), blk(2 * ng), blk(3 * ng), vec, vec],
        out_specs=pl.BlockSpec((lblk, gw), col(0)),
        out_shape=jax.ShapeDtypeStruct((t, HG_WIDTH), BF16),
        scratch_shapes=[pltpu.VMEM((HG_GROUP, HG_DIM, HG_DIM), F32)],
        compiler_params=_params("arbitrary", "arbitrary", "arbitrary"),
        name="hgrn2",
    )(proj, proj, proj, proj, lb.reshape(ng, 1, gw), g_hg.reshape(ng, 1, gw))


def _dsa_prep_kernel(cq_ref, ckv_ref, ki_ref, gcq_ref, gckv_ref, wt_ref,
                     qt_ref, qit_ref, kidx_ref, ckvo_ref, ckvt_ref, wt_o_ref):
    cq = cq_ref[...]
    cqn = cq * lax.rsqrt(jnp.mean(cq * cq, axis=-1, keepdims=True) + 1e-6) * gcq_ref[...]
    cqn_t = cqn.T.astype(BF16)
    nq = SA_HEADS * SA_LATENT
    step = 512
    for c0 in range(0, nq, step):
        qt_ref[c0:c0 + step, :] = _dot(wt_ref[c0:c0 + step, :], cqn_t).astype(BF16)
    for c0 in range(0, IDX_HEADS * IDX_DIM, step):
        qit_ref[c0:c0 + step, :] = _dot(wt_ref[nq + c0:nq + c0 + step, :], cqn_t).astype(BF16)

    ckv = ckv_ref[...]
    ckvn = ckv * lax.rsqrt(jnp.mean(ckv * ckv, axis=-1, keepdims=True) + 1e-6) * gckv_ref[...]
    ckvo_ref[...] = ckvn.astype(BF16)
    ckvt_ref[0:SA_LATENT, :] = ckvn.T.astype(BF16)
    ckvt_ref[SA_LATENT:KVT_ROWS, :] = jnp.ones((KVT_ROWS - SA_LATENT, ckvt_ref.shape[1]), BF16)

    ki = ki_ref[...]
    kidx_ref[...] = ki.astype(BF16)
    ki_t = ki.T
    wt_o_ref[...] = ki_t[IDX_DIM:IDX_DIM + IDX_HEADS, :] * (IDX_HEADS ** -0.5 * IDX_DIM ** -0.5)


def _dsa_prep(proj, g_cq, g_ckv, w_qt, layer, batch, seq, col_cq, tm):
    nq = SA_HEADS * SA_LATENT
    ni = IDX_HEADS * IDX_DIM
    per_b = seq // tm
    cq_blk = col_cq // SA_Q_RANK
    ckv_blk = (col_cq + SA_Q_RANK) // SA_LATENT
    ki_blk = (col_cq + SA_Q_RANK + SA_LATENT) // KI_WIDTH
    tok = lambda i: (i // per_b, i % per_b)
    return pl.pallas_call(
        _dsa_prep_kernel,
        grid=(batch * per_b,),
        in_specs=[
            pl.BlockSpec((tm, SA_Q_RANK), lambda i: (i, cq_blk)),
            pl.BlockSpec((tm, SA_LATENT), lambda i: (i, ckv_blk)),
            pl.BlockSpec((tm, KI_WIDTH), lambda i: (i, ki_blk)),
            pl.BlockSpec((1, SA_Q_RANK), lambda i: (0, 0)),
            pl.BlockSpec((1, SA_LATENT), lambda i: (0, 0)),
            pl.BlockSpec((None, nq + ni, SA_Q_RANK), lambda i: (layer, 0, 0)),
        ],
        out_specs=[
            pl.BlockSpec((None, nq, tm), lambda i: (tok(i)[0], 0, tok(i)[1])),
            pl.BlockSpec((None, ni, tm), lambda i: (tok(i)[0], 0, tok(i)[1])),
            pl.BlockSpec((None, tm, KI_WIDTH), lambda i: (tok(i)[0], tok(i)[1], 0)),
            pl.BlockSpec((None, tm, SA_LATENT), lambda i: (tok(i)[0], tok(i)[1], 0)),
            pl.BlockSpec((None, None, KVT_ROWS, tm), lambda i: (tok(i)[0], tok(i)[1], 0, 0)),
            pl.BlockSpec((None, IDX_HEADS, tm), lambda i: (tok(i)[0], 0, tok(i)[1])),
        ],
        out_shape=[
            jax.ShapeDtypeStruct((batch, nq, seq), BF16),
            jax.ShapeDtypeStruct((batch, ni, seq), BF16),
            jax.ShapeDtypeStruct((batch, seq, KI_WIDTH), BF16),
            jax.ShapeDtypeStruct((batch, seq, SA_LATENT), BF16),
            jax.ShapeDtypeStruct((batch, per_b, KVT_ROWS, tm), BF16),
            jax.ShapeDtypeStruct((batch, IDX_HEADS, seq), F32),
        ],
        compiler_params=_params("arbitrary"),
        name="dsa_prep",
    )(proj, proj, proj, g_cq.reshape(1, -1), g_ckv.reshape(1, -1), w_qt)


def _attn_kernel(qt_ref, qit_ref, wt_ref, kidx_ref, ckv_ref, ckvt_ref, wuvt_ref, y_ref,
                 qcat, qicat, keys, half, thr_scr, pbuf, acc, m_scr, a_scr, ybuf, *, n_sel, slopes):
    qb = pl.program_id(1)
    lanes = Q_BLOCK
    hl = SA_HEADS * lanes

    for h in range(SA_HEADS):
        qcat[:, h * lanes:(h + 1) * lanes] = qt_ref[h * SA_LATENT:(h + 1) * SA_LATENT, :]
        qicat[0:IDX_DIM, h * lanes:(h + 1) * lanes] = qit_ref[h * IDX_DIM:(h + 1) * IDX_DIM, :]
    qicat[IDX_DIM:KI_WIDTH, :] = jnp.zeros((KI_WIDTH - IDX_DIM, hl), BF16)

    n_att = ((qb + 1) * Q_BLOCK + ATT_TK - 1) // ATT_TK
    q_chunk = jnp.right_shift(qb * Q_BLOCK + lax.broadcasted_iota(jnp.int32, (1, lanes), 1), CHUNK_SHIFT)

    def idx_body(j, carry):
        off = pl.multiple_of(j * IDX_TK, IDX_TK)
        kt = kidx_ref[pl.ds(off, IDX_TK), :]
        sc = jnp.zeros((IDX_TK, lanes), F32)
        for hp in range(IDX_HEADS // 2):
            z = _dot(kt, qicat[:, hp * 2 * lanes:(hp + 1) * 2 * lanes])
            for u in range(2):
                h = 2 * hp + u
                sc = sc + wt_ref[h:h + 1, :] * jnp.maximum(z[:, u * lanes:(u + 1) * lanes], 0.0)
        k_chunk = jnp.right_shift(off + lax.broadcasted_iota(jnp.int32, (IDX_TK, lanes), 0), CHUNK_SHIFT)
        bits = lax.bitcast_convert_type(sc, jnp.int32)
        key = jnp.where(bits < 0, bits ^ 0x7FFFFFFF, bits)
        key = jnp.where(k_chunk <= q_chunk, key, INT_MIN)
        keys[pl.ds(off, IDX_TK), :] = key
        half[pl.ds(off, IDX_TK), :] = jnp.right_shift(key, 16).astype(jnp.int16)
        return carry

    lax.fori_loop(0, n_att * (ATT_TK // IDX_TK), idx_body, 0)

    i16_min = -(2 ** 15)

    def tree_sum(parts):
        while len(parts) > 1:
            parts = [a + b for a, b in zip(parts[0::2], parts[1::2])] + parts[len(parts) & ~1:]
        return parts[0]

    def threshold_search(n_tiles):
        n_rows = n_tiles * ATT_TK

        def search16(need):
            def bit_body(it, t):
                cand = t + jnp.left_shift(jnp.int32(1), 15 - it)
                cand16 = jnp.broadcast_to(cand, (16, lanes)).astype(jnp.int16)
                parts = [jnp.zeros((16, lanes), jnp.int16)] * COUNT_CHAINS
                for i, r0 in enumerate(range(0, n_rows, 16)):
                    ge = jnp.where(half[r0:r0 + 16, :] >= cand16, jnp.int16(1), jnp.int16(0))
                    parts[i % COUNT_CHAINS] = parts[i % COUNT_CHAINS] + ge
                cnt = jnp.sum(tree_sum(parts).astype(jnp.int32), axis=0, keepdims=True)
                return jnp.where(cnt >= need, cand, t)

            return lax.fori_loop(0, 16, bit_body, jnp.full((1, lanes), i16_min, jnp.int32))

        t_hi = search16(jnp.full((1, lanes), n_sel, jnp.int32))
        above = []
        for r0 in range(0, n_rows, 64):
            k = keys[r0:r0 + 64, :]
            hi = jnp.right_shift(k, 16)
            lo = (k & 0xFFFF) + i16_min
            half[r0:r0 + 64, :] = jnp.where(hi == t_hi, lo, i16_min).astype(jnp.int16)
            above.append(jnp.sum(jnp.where(hi > t_hi, 1, 0).reshape(8, 8, lanes), axis=0))
        t_lo = search16(n_sel - jnp.sum(tree_sum(above), axis=0, keepdims=True))
        thr = jnp.left_shift(t_hi, 16) + (t_lo - i16_min)
        thr_scr[...] = jnp.maximum(thr, INT_MIN + 1)

    for n_tiles in range(1, keys.shape[0] // ATT_TK + 1):
        pl.when(n_att == n_tiles)(functools.partial(threshold_search, n_tiles))
    thr = thr_scr[...]

    m_scr[...] = jnp.full((1, hl), M_INIT, F32)
    acc[...] = jnp.zeros((KVT_ROWS, hl), F32)
    q_pos = qb * Q_BLOCK + lax.broadcasted_iota(jnp.int32, (ATT_TK, lanes), 1)

    def softmax_tile(j):
        off = pl.multiple_of(j * ATT_TK, ATT_TK)
        kv = ckv_ref[pl.ds(off, ATT_TK), :]
        k_pos = off + lax.broadcasted_iota(jnp.int32, (ATT_TK, lanes), 0)
        dist = jnp.where(keys[pl.ds(off, ATT_TK), :] >= thr, jnp.abs(q_pos - k_pos).astype(F32), MASKED_DIST)
        for hp in range(SA_HEADS // 2):
            s2 = _dot(kv, qcat[:, hp * 2 * lanes:(hp + 1) * 2 * lanes])
            for u in range(2):
                h = 2 * hp + u
                lo = h * lanes
                s = s2[:, u * lanes:(u + 1) * lanes] - slopes[h] * dist
                m_old = m_scr[:, lo:lo + lanes]
                m_new = jnp.maximum(m_old, jnp.max(s, axis=0, keepdims=True))
                a_scr[:, lo:lo + lanes] = jnp.exp2(m_old - m_new)
                m_scr[:, lo:lo + lanes] = m_new
                pbuf[:, lo:lo + lanes] = jnp.exp2(s - m_new).astype(BF16)

    def value_tile(j):
        kvt = ckvt_ref[j]
        for hp in range(SA_HEADS // 2):
            cs = slice(hp * 2 * lanes, (hp + 1) * 2 * lanes)
            acc[:, cs] = acc[:, cs] * a_scr[:, cs] + _dot(kvt, pbuf[:, cs])

    def att_body(j, carry):
        softmax_tile(j)
        value_tile(j)
        return carry

    lax.fori_loop(0, n_att, att_body, 0)

    for h in range(SA_HEADS):
        lo = h * lanes
        o = acc[0:SA_LATENT, lo:lo + lanes] / acc[SA_LATENT:SA_LATENT + 1, lo:lo + lanes]
        ybuf[h * SA_V_DIM:(h + 1) * SA_V_DIM, :] = _dot(wuvt_ref[h], o.astype(BF16))
    y_ref[...] = ybuf[...].T.astype(BF16)


def _attn(qt, qit, wt, kidx, ckv, ckvt, w_uvt, layer, batch, seq):
    nq = SA_HEADS * SA_LATENT
    ni = IDX_HEADS * IDX_DIM
    nblk = seq // Q_BLOCK
    hl = SA_HEADS * Q_BLOCK
    n_sel = min(TOPK_MAX, seq // 4)
    slopes = tuple(float(v) * LOG2E
                   for v in np.exp2(-8.0 * (np.arange(SA_HEADS, dtype=np.float64) + 1.0) / SA_HEADS))
    assert min(slopes) * MASKED_DIST > -2.0 * M_INIT
    return pl.pallas_call(
        functools.partial(_attn_kernel, n_sel=n_sel, slopes=slopes),
        grid=(batch, nblk),
        in_specs=[
            pl.BlockSpec((None, nq, Q_BLOCK), lambda b, i: (b, 0, i)),
            pl.BlockSpec((None, ni, Q_BLOCK), lambda b, i: (b, 0, i)),
            pl.BlockSpec((None, IDX_HEADS, Q_BLOCK), lambda b, i: (b, 0, i)),
            pl.BlockSpec((None, seq, KI_WIDTH), lambda b, i: (b, 0, 0)),
            pl.BlockSpec((None, seq, SA_LATENT), lambda b, i: (b, 0, 0)),
            pl.BlockSpec((None, seq // ATT_TK, KVT_ROWS, ATT_TK), lambda b, i: (b, 0, 0, 0)),
            pl.BlockSpec((None, SA_HEADS, SA_V_DIM, SA_LATENT), lambda b, i: (layer, 0, 0, 0)),
        ],
        out_specs=pl.BlockSpec((Q_BLOCK, SA_WIDTH), lambda b, i: (b * nblk + i, 0)),
        out_shape=jax.ShapeDtypeStruct((batch * seq, SA_WIDTH), BF16),
        scratch_shapes=[
            pltpu.VMEM((SA_LATENT, hl), BF16),
            pltpu.VMEM((KI_WIDTH, hl), BF16),
            pltpu.VMEM((seq, Q_BLOCK), jnp.int32),
            pltpu.VMEM((seq, Q_BLOCK), jnp.int16),
            pltpu.VMEM((1, Q_BLOCK), jnp.int32),
            pltpu.VMEM((ATT_TK, hl), BF16),
            pltpu.VMEM((KVT_ROWS, hl), F32),
            pltpu.VMEM((1, hl), F32),
            pltpu.VMEM((1, hl), F32),
            pltpu.VMEM((SA_WIDTH, Q_BLOCK), F32),
        ],
        compiler_params=_params("arbitrary", "arbitrary"),
        name="dsa_attn",
    )(qt, qit, wt, kidx, ckv, ckvt, w_uvt)


def _merge_kernel(ya_ref, yb_ref, ga_ref, gb_ref, x_ref, gt_ref, wpa_ref, wpb_ref, wout_ref,
                  lng_ref, lnb_ref, o_ref, mbuf, zbuf, *, alpha, tc):
    d = x_ref.shape[1]
    ya = ya_ref[...]
    yb = yb_ref[...]
    for c0 in range(0, d, tc):
        cs = slice(c0, c0 + tc)
        m = (_sigmoid(ga_ref[:, cs]) * _dot(ya, wpa_ref[:, cs])
             + _sigmoid(gb_ref[:, cs]) * _dot(yb, wpb_ref[:, cs]))
        mbuf[:, cs] = m.astype(BF16)
    mm = mbuf[...]
    for c0 in range(0, d, tc):
        cs = slice(c0, c0 + tc)
        zbuf[:, cs] = alpha * x_ref[:, cs] + (1.0 + gt_ref[:, cs]) * _dot(mm, wout_ref[:, cs])
    o_ref[...] = _layer_norm(zbuf[...], lng_ref[...], lnb_ref[...])


def _merge(ya, yb, proj, x2d, gt, wpa, wpb, wout, lng, lnb, layer, seq, col_ga, alpha, tm):
    t, d = x2d.shape
    per_b = seq // tm
    ga_blk = col_ga // d
    const = lambda i: (0, 0)
    wsel = lambda i: (layer, 0, 0)
    return pl.pallas_call(
        functools.partial(_merge_kernel, alpha=alpha, tc=min(d, 512)),
        grid=(t // tm,),
        in_specs=[
            pl.BlockSpec((tm, HG_WIDTH), lambda i: (i, 0)),
            pl.BlockSpec((tm, SA_WIDTH), lambda i: (i, 0)),
            pl.BlockSpec((tm, d), lambda i: (i, ga_blk)),
            pl.BlockSpec((tm, d), lambda i: (i, ga_blk + 1)),
            pl.BlockSpec((tm, d), lambda i: (i, 0)),
            pl.BlockSpec((None, 1, d), lambda i: (i // per_b, 0, 0)),
            pl.BlockSpec((None, HG_WIDTH, d), wsel, pipeline_mode=pl.Buffered(1)),
            pl.BlockSpec((None, SA_WIDTH, d), wsel, pipeline_mode=pl.Buffered(1)),
            pl.BlockSpec((None, d, d), wsel, pipeline_mode=pl.Buffered(1)),
            pl.BlockSpec((1, d), const),
            pl.BlockSpec((1, d), const),
        ],
        out_specs=pl.BlockSpec((tm, d), lambda i: (i, 0)),
        out_shape=jax.ShapeDtypeStruct((t, d), F32),
        scratch_shapes=[pltpu.VMEM((tm, d), BF16), pltpu.VMEM((tm, d), F32)],
        compiler_params=_params("arbitrary"),
        name="merge_out",
    )(ya, yb, proj, proj, x2d, gt, wpa, wpb, wout, lng.reshape(1, d), lnb.reshape(1, d))


def _swiglu_kernel(x_ref, sc_ref, sh_ref, gt_ref, wg_ref, wu_ref, wd_ref, lng_ref, lnb_ref,
                   o_ref, hbuf, acc, *, alpha):
    f = pl.program_id(1)

    @pl.when(f == 0)
    def _():
        hbuf[...] = (x_ref[...] * (1.0 + sc_ref[...]) + sh_ref[...]).astype(BF16)
        acc[...] = jnp.zeros_like(acc)

    h = hbuf[...]
    g = _dot(h, wg_ref[...])
    u = _dot(h, wu_ref[...])
    a = (g * _sigmoid(g) * u).astype(BF16)
    acc[...] += _dot(a, wd_ref[...])

    @pl.when(f == pl.num_programs(1) - 1)
    def _():
        z = alpha * x_ref[...] + (1.0 + gt_ref[...]) * acc[...]
        o_ref[...] = _layer_norm(z, lng_ref[...], lnb_ref[...])


def _swiglu(x2d, sc, sh, gt, wg, wu, wd, lng, lnb, layer, seq, alpha, tm, tf):
    t, d = x2d.shape
    dff = wg.shape[2]
    per_b = seq // tm
    mod = pl.BlockSpec((None, 1, d), lambda i, f: (i // per_b, 0, 0))
    const = pl.BlockSpec((1, d), lambda i, f: (0, 0))
    return pl.pallas_call(
        functools.partial(_swiglu_kernel, alpha=alpha),
        grid=(t // tm, dff // tf),
        in_specs=[
            pl.BlockSpec((tm, d), lambda i, f: (i, 0)),
            mod, mod, mod,
            pl.BlockSpec((None, d, tf), lambda i, f: (layer, 0, f)),
            pl.BlockSpec((None, d, tf), lambda i, f: (layer, 0, f)),
            pl.BlockSpec((None, tf, d), lambda i, f: (layer, f, 0)),
            const, const,
        ],
        out_specs=pl.BlockSpec((tm, d), lambda i, f: (i, 0)),
        out_shape=jax.ShapeDtypeStruct((t, d), F32),
        scratch_shapes=[pltpu.VMEM((tm, d), BF16), pltpu.VMEM((tm, d), F32)],
        compiler_params=_params("arbitrary", "arbitrary"),
        name="swiglu",
    )(x2d, sc, sh, gt, wg, wu, wd, lng.reshape(1, d), lnb.reshape(1, d))


def _largest_tile(n, cap, quantum):
    best = quantum
    for cand in range(quantum, min(n, cap) + 1, quantum):
        if n % cand == 0:
            best = cand
    return best


def kernel(x, c, w_ada, b_ada, w_in, lb_logits, g_hg, w_pa, g_cq, g_ckv, w_uq, w_iq, w_uv, w_pb,
           w_out, ln1_g, ln1_b, w_gate, w_up, w_down, ln2_g, ln2_b):
    batch, seq, d = x.shape
    depth = w_ada.shape[0]
    dff = w_gate.shape[2]
    alpha = (2 * depth) ** 0.25
    assert seq % ATT_TK == 0 and d % 256 == 0 and batch <= ADA_PAD_ROWS

    hg4 = 4 * HG_WIDTH
    col_ga = hg4
    col_cq = hg4 + 2 * d
    used = col_cq + SA_Q_RANK + SA_LATENT + KI_WIDTH
    tn = 1024 if d >= 1024 else 512
    n_proj = -(-used // tn) * tn
    s = np.cumsum([0, HG_WIDTH, HG_WIDTH, HG_WIDTH, HG_WIDTH, SA_Q_RANK, SA_LATENT, IDX_DIM, IDX_HEADS, d, d])
    w_in_r = jnp.concatenate(
        [w_in[:, :, :s[4]], w_in[:, :, s[8]:s[10]], w_in[:, :, s[4]:s[8]],
         jnp.zeros((depth, d, n_proj - (s[10] - s[0])), w_in.dtype)], axis=2).astype(BF16)

    scale = SA_LATENT ** -0.5 * LOG2E
    w_qt =jnp.concatenate([jnp.swapaxes(w_uq, 1, 2) * scale, jnp.swapaxes(w_iq, 1, 2)], axis=1).astype(BF16)
    w_uvt = jnp.swapaxes(w_uv, 2, 3).astype(BF16)
    w_pa_b, w_pb_b, w_out_b = w_pa.astype(BF16), w_pb.astype(BF16), w_out.astype(BF16)
    w_gate_b, w_up_b, w_down_b = w_gate.astype(BF16), w_up.astype(BF16), w_down.astype(BF16)

    p = jax.nn.softmax(lb_logits.astype(F32), axis=0)
    lbs = jnp.cumsum(p, axis=0) - p[0:1]

    c_pad = jnp.zeros((ADA_PAD_ROWS, d), F32).at[:batch].set(c)
    mods = _ada(c_pad, w_ada, b_ada)[:, :batch].reshape(depth, batch, 6, 1, d)

    tm_proj = _largest_tile(seq, 1024, 256)
    lblk = _largest_tile(seq, 1024, HGRN_SUB)
    tm_prep = ATT_TK
    tm_merge = 256
    tm_ffn = _largest_tile(seq, 512, 256)
    tf = _largest_tile(dff, 512, 128)

    x2d = x.reshape(batch * seq, d)
    for l in range(depth):
        sh_m, sc_m, gt_m, sh_f, sc_f, gt_f = (mods[l, :, k] for k in range(6))
        proj = _inproj(x2d, sc_m, sh_m, w_in_r, l, seq, tm_proj, tn)
        y_a = _hgrn(proj, lbs[l], g_hg[l], batch, seq, lblk)
        qt, qit, kidx, ckv, ckvt, wt = _dsa_prep(proj, g_cq[l], g_ckv[l], w_qt, l, batch, seq, col_cq, tm_prep)
        y_b = _attn(qt, qit, wt, kidx, ckv, ckvt, w_uvt, l, batch, seq)
        x2d = _merge(y_a, y_b, proj, x2d, gt_m, w_pa_b, w_pb_b, w_out_b, ln1_g[l], ln1_b[l],
                     l, seq, col_ga, alpha, tm_merge)
        x2d = _swiglu(x2d, sc_f, sh_f, gt_f, w_gate_b, w_up_b, w_down_b, ln2_g[l], ln2_b[l],
                      l, seq, alpha, tm_ffn, tf)
    return x2d.reshape(batch, seq, d)
```

```python
import functools

import numpy as np
import jax
import jax.numpy as jnp
from jax import lax
from jax.experimental import pallas as pl
from jax.experimental.pallas import tpu as pltpu

F32 = jnp.float32
BF16 = jnp.bfloat16

HG_HEADS = 8
HG_DIM = 128
F_MIN = 1e-6
SA_HEADS = 16
SA_LATENT = 256
SA_Q_RANK = 512
SA_V_DIM = 64
IDX_HEADS = 16
IDX_DIM = 64
TOPK_MAX = 256
CHUNK = 64
CHUNK_SHIFT = 6
Q_BLOCK = 128
ADA_PAD_ROWS = 8

HG_WIDTH = HG_HEADS * HG_DIM
SA_WIDTH = SA_HEADS * SA_V_DIM
KI_WIDTH = 128

VMEM_LIMIT = 56 * 1024 * 1024
INT_MIN = -(2 ** 31)
MASKED_DIST = 1e32
M_INIT = -1e29

HGRN_SUB = 256
HG_GROUP = 4
ATT_TK = 512
IDX_TK = 512
COUNT_CHAINS = 8
RESCALE_GUARD = 100.0
KVT_ROWS = SA_LATENT + 16
LOG2E = 1.4426950408889634


def _dot(a, b):
    return jnp.dot(a, b, preferred_element_type=F32)


def _dot_nt(a, b):
    return lax.dot_general(a, b, (((1,), (1,)), ((), ())), preferred_element_type=F32)


def _dot_tn(a, b):
    return lax.dot_general(a, b, (((0,), (0,)), ((), ())), preferred_element_type=F32)


def _sigmoid(x):
    return 1.0 / (1.0 + jnp.exp(-x))


def _layer_norm(z, g, b):
    mu = jnp.mean(z, axis=-1, keepdims=True)
    zc = z - mu
    var = jnp.mean(zc * zc, axis=-1, keepdims=True)
    return zc * lax.rsqrt(var + 1e-5) * g + b


def _params(*sem):
    return pltpu.CompilerParams(dimension_semantics=sem, vmem_limit_bytes=VMEM_LIMIT)


def _ada_kernel(c_ref, w_ref, b_ref, o_ref):
    c = c_ref[...]
    cond = c * _sigmoid(c)
    w = w_ref[...]
    c_hi = cond.astype(BF16)
    c_lo = (cond - c_hi.astype(F32)).astype(BF16)
    w_hi = w.astype(BF16)
    w_lo = (w - w_hi.astype(F32)).astype(BF16)
    acc = _dot(c_hi, w_hi) + _dot(c_hi, w_lo) + _dot(c_lo, w_hi)
    o_ref[...] = acc + b_ref[...]


def _ada(c_pad, w_ada, b_ada):
    depth, d, n = w_ada.shape
    tn = _largest_tile(n, 1024, 128)
    return pl.pallas_call(
        _ada_kernel,
        grid=(depth, n // tn),
        in_specs=[
            pl.BlockSpec((ADA_PAD_ROWS, d), lambda l, j: (0, 0)),
            pl.BlockSpec((None, d, tn), lambda l, j: (l, 0, j)),
            pl.BlockSpec((None, 1, tn), lambda l, j: (l, 0, j)),
        ],
        out_specs=pl.BlockSpec((None, ADA_PAD_ROWS, tn), lambda l, j: (l, 0, j)),
        out_shape=jax.ShapeDtypeStruct((depth, ADA_PAD_ROWS, n), F32),
        compiler_params=_params("arbitrary", "arbitrary"),
        name="ada_mod",
    )(c_pad, w_ada, b_ada.reshape(depth, 1, n))


def _inproj_kernel(x_ref, sc_ref, sh_ref, w_ref, o_ref, h_ref):
    @pl.when(pl.program_id(1) == 0)
    def _():
        h_ref[...] = (x_ref[...] * (1.0 + sc_ref[...]) + sh_ref[...]).astype(BF16)

    o_ref[...] = _dot(h_ref[...], w_ref[...])


def _inproj(x2d, sc, sh, w, layer, seq, tm, tn):
    t, d = x2d.shape
    n = w.shape[2]
    per_b = seq // tm
    return pl.pallas_call(
        _inproj_kernel,
        grid=(t // tm, n // tn),
        in_specs=[
            pl.BlockSpec((tm, d), lambda i, j: (i, 0)),
            pl.BlockSpec((None, 1, d), lambda i, j: (i // per_b, 0, 0)),
            pl.BlockSpec((None, 1, d), lambda i, j: (i // per_b, 0, 0)),
            pl.BlockSpec((None, d, tn), lambda i, j: (layer, 0, j)),
        ],
        out_specs=pl.BlockSpec((tm, tn), lambda i, j: (i, j)),
        out_shape=jax.ShapeDtypeStruct((t, n), F32),
        scratch_shapes=[pltpu.VMEM((tm, d), BF16)],
        compiler_params=_params("arbitrary", "arbitrary"),
        name="in_proj",
    )(x2d, sc, sh, w)


def _hgrn_kernel(q_ref, f_ref, v_ref, g_ref, lb_ref, gh_ref, y_ref, st_ref, *, nsub):
    sub = HGRN_SUB

    @pl.when(pl.program_id(2) == 0)
    def _():
        st_ref[...] = jnp.zeros_like(st_ref)

    row = lax.broadcasted_iota(jnp.int32, (sub, HG_DIM), 0)
    ti = lax.broadcasted_iota(jnp.int32, (sub, sub), 0)
    si = lax.broadcasted_iota(jnp.int32, (sub, sub), 1)
    tx = ti ^ si
    tril = jnp.where(si <= ti, 1.0, 0.0).astype(BF16)
    diag_keep = jnp.where(si <= ti, tx, 8) < 8

    def head_block(off, hh):
        cols = slice(hh * HG_DIM, (hh + 1) * HG_DIM)
        lb = lb_ref[:, cols]
        gh = gh_ref[:, cols]
        q = q_ref[pl.ds(off, sub), cols]
        fa = f_ref[pl.ds(off, sub), cols]
        v = v_ref[pl.ds(off, sub), cols].astype(BF16)
        ga = g_ref[pl.ds(off, sub), cols]

        f = lb + (1.0 - lb) * _sigmoid(fa)
        logf = jnp.log(jnp.clip(f, F_MIN, 1.0))
        kk = 1.0 - f

        l_hi = logf.astype(BF16)
        r1 = logf - l_hi.astype(F32)
        l_mid = r1.astype(BF16)
        l_lo = (r1 - l_mid.astype(F32)).astype(BF16)
        b = _dot(tril, l_hi) + _dot(tril, l_mid) + _dot(tril, l_lo)

        a = jnp.zeros((sub, sub), F32)
        h = sub // 2
        while h >= 8:
            n = sub // (2 * h)
            b3 = b.reshape(n, 2 * h, HG_DIM)
            e = jnp.exp(-jnp.abs(b3 - b3[:, h - 1:h, :])).reshape(sub, HG_DIM)
            second = (row & h) != 0
            qd = jnp.where(second, q * e, 0.0).astype(BF16)
            kd = jnp.where(second, 0.0, kk * e).astype(BF16)
            p = _dot_nt(qd, kd)
            a = a + (p if 2 * h == sub else jnp.where(tx < 2 * h, p, 0.0))
            h //= 2
        b3 = b.reshape(sub // 8, 8, HG_DIM)
        xd = (b3 - b3[:, 3:4, :]).reshape(sub, HG_DIM)
        qd = (q * jnp.exp(xd)).astype(BF16)
        kd = (kk * jnp.exp(-xd)).astype(BF16)
        a = a + jnp.where(diag_keep, _dot_nt(qd, kd), 0.0)

        st = st_ref[hh]
        qi = (q * jnp.exp(b)).astype(BF16)
        o = _dot_nt(qi, st.astype(BF16)) + _dot(a.astype(BF16), v)
        b_last = b[sub - 1:sub, :]
        kdi = (kk * jnp.exp(b_last - b)).astype(BF16)
        st_ref[hh] = st * jnp.exp(b_last) + _dot_tn(v, kdi)

        ms = jnp.mean(o * o, axis=-1, keepdims=True)
        y = o * lax.rsqrt(ms + 1e-6) * gh * (ga * _sigmoid(ga))
        y_ref[pl.ds(off, sub), cols] = y.astype(BF16)

    def body(r, carry):
        off = pl.multiple_of(r * sub, sub)
        for hh in range(HG_GROUP):
            head_block(off, hh)
        return carry

    lax.fori_loop(0, nsub, body, 0)


def _hgrn(proj, lb, g_hg, batch, seq, lblk):
    t = proj.shape[0]
    nl = seq // lblk
    ng = HG_HEADS // HG_GROUP
    gw = HG_GROUP * HG_DIM
    col = lambda c0: (lambda b, h, l: (b * nl + l, c0 + h))
    blk = lambda c0: pl.BlockSpec((lblk, gw), col(c0))
    vec = pl.BlockSpec((None, 1, gw), lambda b, h, l: (h, 0, 0))
    return pl.pallas_call(
        functools.partial(_hgrn_kernel, nsub=lblk // HGRN_SUB),
        grid=(batch, ng, nl),
        in_specs=[blk(0), blk(ng), blk(2 * ng), blk(3 * ng), vec, vec],
        out_specs=pl.BlockSpec((lblk, gw), col(0)),
        out_shape=jax.ShapeDtypeStruct((t, HG_WIDTH), BF16),
        scratch_shapes=[pltpu.VMEM((HG_GROUP, HG_DIM, HG_DIM), F32)],
        compiler_params=_params("arbitrary", "arbitrary", "arbitrary"),
        name="hgrn2",
    )(proj, proj, proj, proj, lb.reshape(ng, 1, gw), g_hg.reshape(ng, 1, gw))


def _dsa_prep_kernel(cq_ref, ckv_ref, ki_ref, gcq_ref, gckv_ref, wt_ref,
                     qt_ref, qit_ref, kidx_ref, ckvo_ref, ckvt_ref, wt_o_ref):
    cq = cq_ref[...]
    cqn = cq * lax.rsqrt(jnp.mean(cq * cq, axis=-1, keepdims=True) + 1e-6) * gcq_ref[...]
    cqn_t = cqn.T.astype(BF16)
    nq = SA_HEADS * SA_LATENT
    step = 512
    for c0 in range(0, nq, step):
        qt_ref[c0:c0 + step, :] = _dot(wt_ref[c0:c0 + step, :], cqn_t).astype(BF16)
    for c0 in range(0, IDX_HEADS * IDX_DIM, step):
        qit_ref[c0:c0 + step, :] = _dot(wt_ref[nq + c0:nq + c0 + step, :], cqn_t).astype(BF16)

    ckv = ckv_ref[...]
    ckvn = ckv * lax.rsqrt(jnp.mean(ckv * ckv, axis=-1, keepdims=True) + 1e-6) * gckv_ref[...]
    ckvo_ref[...] = ckvn.astype(BF16)
    ckvt_ref[0:SA_LATENT, :] = ckvn.T.astype(BF16)
    ckvt_ref[SA_LATENT:KVT_ROWS, :] = jnp.ones((KVT_ROWS - SA_LATENT, ckvt_ref.shape[1]), BF16)

    ki = ki_ref[...]
    kidx_ref[...] = ki.astype(BF16)
    ki_t = ki.T
    wt_o_ref[...] = ki_t[IDX_DIM:IDX_DIM + IDX_HEADS, :] * (IDX_HEADS ** -0.5 * IDX_DIM ** -0.5)


def _dsa_prep(proj, g_cq, g_ckv, w_qt, layer, batch, seq, col_cq, tm):
    nq = SA_HEADS * SA_LATENT
    ni = IDX_HEADS * IDX_DIM
    per_b = seq // tm
    cq_blk = col_cq // SA_Q_RANK
    ckv_blk = (col_cq + SA_Q_RANK) // SA_LATENT
    ki_blk = (col_cq + SA_Q_RANK + SA_LATENT) // KI_WIDTH
    tok = lambda i: (i // per_b, i % per_b)
    return pl.pallas_call(
        _dsa_prep_kernel,
        grid=(batch * per_b,),
        in_specs=[
            pl.BlockSpec((tm, SA_Q_RANK), lambda i: (i, cq_blk)),
            pl.BlockSpec((tm, SA_LATENT), lambda i: (i, ckv_blk)),
            pl.BlockSpec((tm, KI_WIDTH), lambda i: (i, ki_blk)),
            pl.BlockSpec((1, SA_Q_RANK), lambda i: (0, 0)),
            pl.BlockSpec((1, SA_LATENT), lambda i: (0, 0)),
            pl.BlockSpec((None, nq + ni, SA_Q_RANK), lambda i: (layer, 0, 0)),
        ],
        out_specs=[
            pl.BlockSpec((None, nq, tm), lambda i: (tok(i)[0], 0, tok(i)[1])),
            pl.BlockSpec((None, ni, tm), lambda i: (tok(i)[0], 0, tok(i)[1])),
            pl.BlockSpec((None, tm, KI_WIDTH), lambda i: (tok(i)[0], tok(i)[1], 0)),
            pl.BlockSpec((None, tm, SA_LATENT), lambda i: (tok(i)[0], tok(i)[1], 0)),
            pl.BlockSpec((None, None, KVT_ROWS, tm), lambda i: (tok(i)[0], tok(i)[1], 0, 0)),
            pl.BlockSpec((None, IDX_HEADS, tm), lambda i: (tok(i)[0], 0, tok(i)[1])),
        ],
        out_shape=[
            jax.ShapeDtypeStruct((batch, nq, seq), BF16),
            jax.ShapeDtypeStruct((batch, ni, seq), BF16),
            jax.ShapeDtypeStruct((batch, seq, KI_WIDTH), BF16),
            jax.ShapeDtypeStruct((batch, seq, SA_LATENT), BF16),
            jax.ShapeDtypeStruct((batch, per_b, KVT_ROWS, tm), BF16),
            jax.ShapeDtypeStruct((batch, IDX_HEADS, seq), F32),
        ],
        compiler_params=_params("arbitrary"),
        name="dsa_prep",
    )(proj, proj, proj, g_cq.reshape(1, -1), g_ckv.reshape(1, -1), w_qt)


def _attn_kernel(qt_ref, qit_ref, wt_ref, kidx_ref, ckv_ref, ckvt_ref, wuvt_ref, y_ref,
                 qcat, qicat, keys, half, thr_scr, pbuf, acc, m_scr, a_scr, rise_scr, ybuf, *, n_sel, slopes):
    qb = pl.program_id(1)
    lanes = Q_BLOCK
    hl = SA_HEADS * lanes

    for h in range(SA_HEADS):
        qcat[:, h * lanes:(h + 1) * lanes] = qt_ref[h * SA_LATENT:(h + 1) * SA_LATENT, :]
        qicat[0:IDX_DIM, h * lanes:(h + 1) * lanes] = qit_ref[h * IDX_DIM:(h + 1) * IDX_DIM, :]
    qicat[IDX_DIM:KI_WIDTH, :] = jnp.zeros((KI_WIDTH - IDX_DIM, hl), BF16)

    per_tile = ATT_TK // Q_BLOCK
    n_full = (qb + 1) // per_tile
    rem = (qb + 1) % per_tile
    n_att = n_full + jnp.minimum(rem, 1)
    q_chunk = jnp.right_shift(qb * Q_BLOCK + lax.broadcasted_iota(jnp.int32, (1, lanes), 1), CHUNK_SHIFT)
    i16_min = -(2 ** 15)

    def idx_tile(j, rows):
        off = pl.multiple_of(j * ATT_TK, ATT_TK)
        kt = kidx_ref[pl.ds(off, rows), :]
        sc = jnp.zeros((rows, lanes), F32)
        for hp in range(IDX_HEADS // 2):
            z = _dot(kt, qicat[:, hp * 2 * lanes:(hp + 1) * 2 * lanes])
            for u in range(2):
                h = 2 * hp + u
                sc = sc + wt_ref[h:h + 1, :] * jnp.maximum(z[:, u * lanes:(u + 1) * lanes], 0.0)
        k_chunk = jnp.right_shift(off + lax.broadcasted_iota(jnp.int32, (rows, lanes), 0), CHUNK_SHIFT)
        bits = lax.bitcast_convert_type(sc, jnp.int32)
        key = jnp.where(bits < 0, bits ^ 0x7FFFFFFF, bits)
        key = jnp.where(k_chunk <= q_chunk, key, INT_MIN)
        keys[pl.ds(off, rows), :] = key
        half[pl.ds(off, rows), :] = jnp.right_shift(key, 16).astype(jnp.int16)
        if rows < ATT_TK:
            keys[pl.ds(off + rows, ATT_TK - rows), :] = jnp.full((ATT_TK - rows, lanes), INT_MIN, jnp.int32)
            half[pl.ds(off + rows, ATT_TK - rows), :] = jnp.full((ATT_TK - rows, lanes), i16_min, jnp.int16)

    def idx_body(j, carry):
        idx_tile(j, ATT_TK)
        return carry

    lax.fori_loop(0, n_full, idx_body, 0)
    for v in range(1, per_tile):
        pl.when(rem == v)(functools.partial(idx_tile, n_full, v * Q_BLOCK))


    def tree_sum(parts):
        while len(parts) > 1:
            parts = [a + b for a, b in zip(parts[0::2], parts[1::2])] + parts[len(parts) & ~1:]
        return parts[0]

    def threshold_search(n_tiles):
        n_rows = n_tiles * ATT_TK

        def search16(need):
            def bit_body(it, t):
                cand = t + jnp.left_shift(jnp.int32(1), 15 - it)
                cand16 = jnp.broadcast_to(cand, (16, lanes)).astype(jnp.int16)
                parts = [jnp.zeros((16, lanes), jnp.int16)] * COUNT_CHAINS
                for i, r0 in enumerate(range(0, n_rows, 16)):
                    ge = jnp.where(half[r0:r0 + 16, :] >= cand16, jnp.int16(1), jnp.int16(0))
                    parts[i % COUNT_CHAINS] = parts[i % COUNT_CHAINS] + ge
                cnt = jnp.sum(tree_sum(parts).astype(jnp.int32), axis=0, keepdims=True)
                return jnp.where(cnt >= need, cand, t)

            return lax.fori_loop(0, 16, bit_body, jnp.full((1, lanes), i16_min, jnp.int32))

        t_hi = search16(jnp.full((1, lanes), n_sel, jnp.int32))
        above = []
        for r0 in range(0, n_rows, 64):
            k = keys[r0:r0 + 64, :]
            hi = jnp.right_shift(k, 16)
            lo = (k & 0xFFFF) + i16_min
            half[r0:r0 + 64, :] = jnp.where(hi == t_hi, lo, i16_min).astype(jnp.int16)
            above.append(jnp.sum(jnp.where(hi > t_hi, 1, 0).reshape(8, 8, lanes), axis=0))
        t_lo = search16(n_sel - jnp.sum(tree_sum(above), axis=0, keepdims=True))
        thr = jnp.left_shift(t_hi, 16) + (t_lo - i16_min)
        thr_scr[...] = jnp.maximum(thr, INT_MIN + 1)

    for n_tiles in range(1, keys.shape[0] // ATT_TK + 1):
        pl.when(n_att == n_tiles)(functools.partial(threshold_search, n_tiles))
    thr = thr_scr[...]

    def scores(j, rows):
        off = pl.multiple_of(j * ATT_TK, ATT_TK)
        kv = ckv_ref[pl.ds(off, rows), :]
        q_pos = qb * Q_BLOCK + lax.broadcasted_iota(jnp.int32, (rows, lanes), 1)
        k_pos = off + lax.broadcasted_iota(jnp.int32, (rows, lanes), 0)
        dist = jnp.where(keys[pl.ds(off, rows), :] >= thr, jnp.abs(q_pos - k_pos).astype(F32), MASKED_DIST)
        for hp in range(SA_HEADS // 2):
            s2 = _dot(kv, qcat[:, hp * 2 * lanes:(hp + 1) * 2 * lanes])
            for u in range(2):
                h = 2 * hp + u
                yield h * lanes, s2[:, u * lanes:(u + 1) * lanes] - slopes[h] * dist

    def reset():
        m_scr[...] = jnp.full((1, hl), M_INIT, F32)
        acc[...] = jnp.zeros((KVT_ROWS, hl), F32)

    def safe_tile(j, rows=ATT_TK):
        for lo, s in scores(j, rows):
            m_old = m_scr[:, lo:lo + lanes]
            m_new = jnp.maximum(m_old, jnp.max(s, axis=0, keepdims=True))
            a_scr[:, lo:lo + lanes] = jnp.exp2(m_old - m_new)
            m_scr[:, lo:lo + lanes] = m_new
            pbuf[0:rows, lo:lo + lanes] = jnp.exp2(s - m_new).astype(BF16)
        kvt = ckvt_ref[j, :, 0:rows]
        for hp in range(SA_HEADS // 2):
            cs = slice(hp * 2 * lanes, (hp + 1) * 2 * lanes)
            acc[:, cs] = acc[:, cs] * a_scr[:, cs] + _dot(kvt, pbuf[0:rows, cs])

    def fast_tile(j):
        for lo, s in scores(j, ATT_TK):
            ref = m_scr[:, lo:lo + lanes]
            pbuf[:, lo:lo + lanes] = jnp.exp2(s - ref).astype(BF16)
            m_tile = jnp.max(s, axis=0, keepdims=True)
            m_new = jnp.maximum(ref, m_tile)
            a_scr[:, lo:lo + lanes] = jnp.exp2(ref - m_new)
            m_scr[:, lo:lo + lanes] = m_new
            rise_scr[:, lo:lo + lanes] = jnp.maximum(rise_scr[:, lo:lo + lanes], m_tile - ref)
        kvt = ckvt_ref[j]
        for hp in range(SA_HEADS // 2):
            cs = slice(hp * 2 * lanes, (hp + 1) * 2 * lanes)
            acc[:, cs] = (acc[:, cs] + _dot(kvt, pbuf[:, cs])) * a_scr[:, cs]

    reset()
    rise_scr[...] = jnp.zeros((1, hl), F32)
    for v in range(1, per_tile):
        pl.when(rem == v)(functools.partial(safe_tile, n_full, v * Q_BLOCK))
    n_two_pass = jnp.minimum(n_full, 2 - jnp.minimum(rem, 1))

    def two_pass_body(i, carry):
        safe_tile(n_full - 1 - i)
        return carry

    def one_pass_body(i, carry):
        fast_tile(n_full - 1 - n_two_pass - i)
        return carry

    lax.fori_loop(0, n_two_pass, two_pass_body, 0)
    lax.fori_loop(0, n_full - n_two_pass, one_pass_body, 0)

    @pl.when(jnp.max(rise_scr[...]) > RESCALE_GUARD)
    def _():
        reset()

        def redo_body(j, carry):
            safe_tile(j)
            return carry

        lax.fori_loop(0, n_att, redo_body, 0)

    for h in range(SA_HEADS):
        lo = h * lanes
        o = acc[0:SA_LATENT, lo:lo + lanes] / acc[SA_LATENT:SA_LATENT + 1, lo:lo + lanes]
        ybuf[h * SA_V_DIM:(h + 1) * SA_V_DIM, :] = _dot(wuvt_ref[h], o.astype(BF16))
    y_ref[...] = ybuf[...].T.astype(BF16)


def _attn(qt, qit, wt, kidx, ckv, ckvt, w_uvt, layer, batch, seq):
    nq = SA_HEADS * SA_LATENT
    ni = IDX_HEADS * IDX_DIM
    nblk = seq // Q_BLOCK
    hl = SA_HEADS * Q_BLOCK
    n_sel = min(TOPK_MAX, seq // 4)
    slopes = tuple(float(v) * LOG2E
                   for v in np.exp2(-8.0 * (np.arange(SA_HEADS, dtype=np.float64) + 1.0) / SA_HEADS))
    assert min(slopes) * MASKED_DIST > -2.0 * M_INIT
    return pl.pallas_call(
        functools.partial(_attn_kernel, n_sel=n_sel, slopes=slopes),
        grid=(batch, nblk),
        in_specs=[
            pl.BlockSpec((None, nq, Q_BLOCK), lambda b, i: (b, 0, i)),
            pl.BlockSpec((None, ni, Q_BLOCK), lambda b, i: (b, 0, i)),
            pl.BlockSpec((None, IDX_HEADS, Q_BLOCK), lambda b, i: (b, 0, i)),
            pl.BlockSpec((None, seq, KI_WIDTH), lambda b, i: (b, 0, 0)),
            pl.BlockSpec((None, seq, SA_LATENT), lambda b, i: (b, 0, 0)),
            pl.BlockSpec((None, seq // ATT_TK, KVT_ROWS, ATT_TK), lambda b, i: (b, 0, 0, 0)),
            pl.BlockSpec((None, SA_HEADS, SA_V_DIM, SA_LATENT), lambda b, i: (layer, 0, 0, 0)),
        ],
        out_specs=pl.BlockSpec((Q_BLOCK, SA_WIDTH), lambda b, i: (b * nblk + i, 0)),
        out_shape=jax.ShapeDtypeStruct((batch * seq, SA_WIDTH), BF16),
        scratch_shapes=[
            pltpu.VMEM((SA_LATENT, hl), BF16),
            pltpu.VMEM((KI_WIDTH, hl), BF16),
            pltpu.VMEM((seq, Q_BLOCK), jnp.int32),
            pltpu.VMEM((seq, Q_BLOCK), jnp.int16),
            pltpu.VMEM((1, Q_BLOCK), jnp.int32),
            pltpu.VMEM((ATT_TK, hl), BF16),
            pltpu.VMEM((KVT_ROWS, hl), F32),
            pltpu.VMEM((1, hl), F32),
            pltpu.VMEM((1, hl), F32),
            pltpu.VMEM((1, hl), F32),
            pltpu.VMEM((SA_WIDTH, Q_BLOCK), F32),
        ],
        compiler_params=_params("arbitrary", "arbitrary"),
        name="dsa_attn",
    )(qt, qit, wt, kidx, ckv, ckvt, w_uvt)


def _merge_kernel(ya_ref, yb_ref, ga_ref, gb_ref, x_ref, gt_ref, wpa_ref, wpb_ref, wout_ref,
                  lng_ref, lnb_ref, o_ref, mbuf, zbuf, *, alpha, tc):
    d = x_ref.shape[1]
    ya = ya_ref[...]
    yb = yb_ref[...]
    for c0 in range(0, d, tc):
        cs = slice(c0, c0 + tc)
        m = (_sigmoid(ga_ref[:, cs]) * _dot(ya, wpa_ref[:, cs])
             + _sigmoid(gb_ref[:, cs]) * _dot(yb, wpb_ref[:, cs]))
        mbuf[:, cs] = m.astype(BF16)
    mm = mbuf[...]
    for c0 in range(0, d, tc):
        cs = slice(c0, c0 + tc)
        zbuf[:, cs] = alpha * x_ref[:, cs] + (1.0 + gt_ref[:, cs]) * _dot(mm, wout_ref[:, cs])
    o_ref[...] = _layer_norm(zbuf[...], lng_ref[...], lnb_ref[...])


def _merge(ya, yb, proj, x2d, gt, wpa, wpb, wout, lng, lnb, layer, seq, col_ga, alpha, tm):
    t, d = x2d.shape
    per_b = seq // tm
    ga_blk = col_ga // d
    const = lambda i: (0, 0)
    wsel = lambda i: (layer, 0, 0)
    return pl.pallas_call(
        functools.partial(_merge_kernel, alpha=alpha, tc=min(d, 512)),
        grid=(t // tm,),
        in_specs=[
            pl.BlockSpec((tm, HG_WIDTH), lambda i: (i, 0)),
            pl.BlockSpec((tm, SA_WIDTH), lambda i: (i, 0)),
            pl.BlockSpec((tm, d), lambda i: (i, ga_blk)),
            pl.BlockSpec((tm, d), lambda i: (i, ga_blk + 1)),
            pl.BlockSpec((tm, d), lambda i: (i, 0)),
            pl.BlockSpec((None, 1, d), lambda i: (i // per_b, 0, 0)),
            pl.BlockSpec((None, HG_WIDTH, d), wsel, pipeline_mode=pl.Buffered(1)),
            pl.BlockSpec((None, SA_WIDTH, d), wsel, pipeline_mode=pl.Buffered(1)),
            pl.BlockSpec((None, d, d), wsel, pipeline_mode=pl.Buffered(1)),
            pl.BlockSpec((1, d), const),
            pl.BlockSpec((1, d), const),
        ],
        out_specs=pl.BlockSpec((tm, d), lambda i: (i, 0)),
        out_shape=jax.ShapeDtypeStruct((t, d), F32),
        scratch_shapes=[pltpu.VMEM((tm, d), BF16), pltpu.VMEM((tm, d), F32)],
        compiler_params=_params("arbitrary"),
        name="merge_out",
    )(ya, yb, proj, proj, x2d, gt, wpa, wpb, wout, lng.reshape(1, d), lnb.reshape(1, d))


def _swiglu_kernel(x_ref, sc_ref, sh_ref, gt_ref, wg_ref, wu_ref, wd_ref, lng_ref, lnb_ref,
                   o_ref, hbuf, acc, *, alpha):
    f = pl.program_id(1)

    @pl.when(f == 0)
    def _():
        hbuf[...] = (x_ref[...] * (1.0 + sc_ref[...]) + sh_ref[...]).astype(BF16)
        acc[...] = jnp.zeros_like(acc)

    h = hbuf[...]
    g = _dot(h, wg_ref[...])
    u = _dot(h, wu_ref[...])
    a = (g * _sigmoid(g) * u).astype(BF16)
    acc[...] += _dot(a, wd_ref[...])

    @pl.when(f == pl.num_programs(1) - 1)
    def _():
        z = alpha * x_ref[...] + (1.0 + gt_ref[...]) * acc[...]
        o_ref[...] = _layer_norm(z, lng_ref[...], lnb_ref[...])


def _swiglu(x2d, sc, sh, gt, wg, wu, wd, lng, lnb, layer, seq, alpha, tm, tf):
    t, d = x2d.shape
    dff = wg.shape[2]
    per_b = seq // tm
    mod = pl.BlockSpec((None, 1, d), lambda i, f: (i // per_b, 0, 0))
    const = pl.BlockSpec((1, d), lambda i, f: (0, 0))
    return pl.pallas_call(
        functools.partial(_swiglu_kernel, alpha=alpha),
        grid=(t // tm, dff // tf),
        in_specs=[
            pl.BlockSpec((tm, d), lambda i, f: (i, 0)),
            mod, mod, mod,
            pl.BlockSpec((None, d, tf), lambda i, f: (layer, 0, f)),
            pl.BlockSpec((None, d, tf), lambda i, f: (layer, 0, f)),
            pl.BlockSpec((None, tf, d), lambda i, f: (layer, f, 0)),
            const, const,
        ],
        out_specs=pl.BlockSpec((tm, d), lambda i, f: (i, 0)),
        out_shape=jax.ShapeDtypeStruct((t, d), F32),
        scratch_shapes=[pltpu.VMEM((tm, d), BF16), pltpu.VMEM((tm, d), F32)],
        compiler_params=_params("arbitrary", "arbitrary"),
        name="swiglu",
    )(x2d, sc, sh, gt, wg, wu, wd, lng.reshape(1, d), lnb.reshape(1, d))


def _largest_tile(n, cap, quantum):
    best = quantum
    for cand in range(quantum, min(n, cap) + 1, quantum):
        if n % cand == 0:
            best = cand
    return best


def kernel(x, c, w_ada, b_ada, w_in, lb_logits, g_hg, w_pa, g_cq, g_ckv, w_uq, w_iq, w_uv, w_pb,
           w_out, ln1_g, ln1_b, w_gate, w_up, w_down, ln2_g, ln2_b):
    batch, seq, d = x.shape
    depth = w_ada.shape[0]
    dff = w_gate.shape[2]
    alpha = (2 * depth) ** 0.25
    assert seq % ATT_TK == 0 and d % 256 == 0 and batch <= ADA_PAD_ROWS

    hg4 = 4 * HG_WIDTH
    col_ga = hg4
    col_cq = hg4 + 2 * d
    used = col_cq + SA_Q_RANK + SA_LATENT + KI_WIDTH
    tn = 1024 if d >= 1024 else 512
    n_proj = -(-used // tn) * tn
    s = np.cumsum([0, HG_WIDTH, HG_WIDTH, HG_WIDTH, HG_WIDTH, SA_Q_RANK, SA_LATENT, IDX_DIM, IDX_HEADS, d, d])
    w_in_r = jnp.concatenate(
        [w_in[:, :, :s[4]], w_in[:, :, s[8]:s[10]], w_in[:, :, s[4]:s[8]],
         jnp.zeros((depth, d, n_proj - (s[10] - s[0])), w_in.dtype)], axis=2).astype(BF16)

    scale = SA_LATENT ** -0.5 * LOG2E
    w_qt =jnp.concatenate([jnp.swapaxes(w_uq, 1, 2) * scale, jnp.swapaxes(w_iq, 1, 2)], axis=1).astype(BF16)
    w_uvt = jnp.swapaxes(w_uv, 2, 3).astype(BF16)
    w_pa_b, w_pb_b, w_out_b = w_pa.astype(BF16), w_pb.astype(BF16), w_out.astype(BF16)
    w_gate_b, w_up_b, w_down_b = w_gate.astype(BF16), w_up.astype(BF16), w_down.astype(BF16)

    p = jax.nn.softmax(lb_logits.astype(F32), axis=0)
    lbs = jnp.cumsum(p, axis=0) - p[0:1]

    c_pad = jnp.zeros((ADA_PAD_ROWS, d), F32).at[:batch].set(c)
    mods = _ada(c_pad, w_ada, b_ada)[:, :batch].reshape(depth, batch, 6, 1, d)

    tm_proj = _largest_tile(seq, 1024, 256)
    lblk = _largest_tile(seq, 1024, HGRN_SUB)
    tm_prep = ATT_TK
    tm_merge = 256
    tm_ffn = _largest_tile(seq, 512, 256)
    tf = _largest_tile(dff, 512, 128)

    x2d = x.reshape(batch * seq, d)
    for l in range(depth):
        sh_m, sc_m, gt_m, sh_f, sc_f, gt_f = (mods[l, :, k] for k in range(6))
        proj = _inproj(x2d, sc_m, sh_m, w_in_r, l, seq, tm_proj, tn)
        y_a = _hgrn(proj, lbs[l], g_hg[l], batch, seq, lblk)
        qt, qit, kidx, ckv, ckvt, wt = _dsa_prep(proj, g_cq[l], g_ckv[l], w_qt, l, batch, seq, col_cq, tm_prep)
        y_b = _attn(qt, qit, wt, kidx, ckv, ckvt, w_uvt, l, batch, seq)
        x2d = _merge(y_a, y_b, proj, x2d, gt_m, w_pa_b, w_pb_b, w_out_b, ln1_g[l], ln1_b[l],
                     l, seq, col_ga, alpha, tm_merge)
        x2d = _swiglu(x2d, sc_f, sh_f, gt_f, w_gate_b, w_up_b, w_down_b, ln2_g[l], ln2_b[l],
                      l, seq, alpha, tm_ffn, tf)
    return x2d.reshape(batch, seq, d)
```

```python
import functools

import numpy as np
import jax
import jax.numpy as jnp
from jax import lax
from jax.experimental import pallas as pl
from jax.experimental.pallas import tpu as pltpu

F32 = jnp.float32
BF16 = jnp.bfloat16

HG_HEADS = 8
HG_DIM = 128
F_MIN = 1e-6
SA_HEADS = 16
SA_LATENT = 256
SA_Q_RANK = 512
SA_V_DIM = 64
IDX_HEADS = 16
IDX_DIM = 64
TOPK_MAX = 256
CHUNK = 64
CHUNK_SHIFT = 6
Q_BLOCK = 128
ADA_PAD_ROWS = 8

HG_WIDTH = HG_HEADS * HG_DIM
SA_WIDTH = SA_HEADS * SA_V_DIM
KI_WIDTH = 128

VMEM_LIMIT = 56 * 1024 * 1024
INT_MIN = -(2 ** 31)
MASKED_DIST = 1e32
M_INIT = -1e29

HGRN_SUB = 256
HG_GROUP = 4
ATT_TK = 512
IDX_TK = 512
COUNT_CHAINS = 8
RESCALE_GUARD = 100.0
KVT_ROWS = SA_LATENT + 16
LOG2E = 1.4426950408889634


def _dot(a, b):
    return jnp.dot(a, b, preferred_element_type=F32)


def _dot_nt(a, b):
    return lax.dot_general(a, b, (((1,), (1,)), ((), ())), preferred_element_type=F32)


def _dot_tn(a, b):
    return lax.dot_general(a, b, (((0,), (0,)), ((), ())), preferred_element_type=F32)


def _sigmoid(x):
    return 1.0 / (1.0 + jnp.exp(-x))


def _layer_norm(z, g, b):
    mu = jnp.mean(z, axis=-1, keepdims=True)
    zc = z - mu
    var = jnp.mean(zc * zc, axis=-1, keepdims=True)
    return zc * lax.rsqrt(var + 1e-5) * g + b


def _params(*sem):
    return pltpu.CompilerParams(dimension_semantics=sem, vmem_limit_bytes=VMEM_LIMIT)


def _ada_kernel(c_ref, w_ref, b_ref, o_ref):
    c = c_ref[...]
    cond = c * _sigmoid(c)
    w = w_ref[...]
    c_hi = cond.astype(BF16)
    c_lo = (cond - c_hi.astype(F32)).astype(BF16)
    w_hi = w.astype(BF16)
    w_lo = (w - w_hi.astype(F32)).astype(BF16)
    acc = _dot(c_hi, w_hi) + _dot(c_hi, w_lo) + _dot(c_lo, w_hi)
    o_ref[...] = acc + b_ref[...]


def _ada(c_pad, w_ada, b_ada):
    depth, d, n = w_ada.shape
    tn = _largest_tile(n, 1024, 128)
    return pl.pallas_call(
        _ada_kernel,
        grid=(depth, n // tn),
        in_specs=[
            pl.BlockSpec((ADA_PAD_ROWS, d), lambda l, j: (0, 0)),
            pl.BlockSpec((None, d, tn), lambda l, j: (l, 0, j)),
            pl.BlockSpec((None, 1, tn), lambda l, j: (l, 0, j)),
        ],
        out_specs=pl.BlockSpec((None, ADA_PAD_ROWS, tn), lambda l, j: (l, 0, j)),
        out_shape=jax.ShapeDtypeStruct((depth, ADA_PAD_ROWS, n), F32),
        compiler_params=_params("arbitrary", "arbitrary"),
        name="ada_mod",
    )(c_pad, w_ada, b_ada.reshape(depth, 1, n))


def _inproj_kernel(x_ref, sc_ref, sh_ref, w_ref, o_ref, h_ref):
    @pl.when(pl.program_id(1) == 0)
    def _():
        h_ref[...] = (x_ref[...] * (1.0 + sc_ref[...]) + sh_ref[...]).astype(BF16)

    o_ref[...] = _dot_nt(h_ref[...], w_ref[...])


def _inproj(x2d, sc, sh, w, layer, seq, tm, tn):
    t, d = x2d.shape
    n = w.shape[1]
    per_b = seq // tm
    return pl.pallas_call(
        _inproj_kernel,
        grid=(t // tm, n // tn),
        in_specs=[
            pl.BlockSpec((tm, d), lambda i, j: (i, 0)),
            pl.BlockSpec((None, 1, d), lambda i, j: (i // per_b, 0, 0)),
            pl.BlockSpec((None, 1, d), lambda i, j: (i // per_b, 0, 0)),
            pl.BlockSpec((None, tn, d), lambda i, j: (layer, j, 0)),
        ],
        out_specs=pl.BlockSpec((tm, tn), lambda i, j: (i, j)),
        out_shape=jax.ShapeDtypeStruct((t, n), F32),
        scratch_shapes=[pltpu.VMEM((tm, d), BF16)],
        compiler_params=_params("arbitrary", "arbitrary"),
        name="in_proj",
    )(x2d, sc, sh, w)


def _hgrn_kernel(q_ref, f_ref, v_ref, g_ref, lb_ref, gh_ref, y_ref, st_ref, *, nsub):
    sub = HGRN_SUB

    @pl.when(pl.program_id(2) == 0)
    def _():
        st_ref[...] = jnp.zeros_like(st_ref)

    row = lax.broadcasted_iota(jnp.int32, (sub, HG_DIM), 0)
    ti = lax.broadcasted_iota(jnp.int32, (sub, sub), 0)
    si = lax.broadcasted_iota(jnp.int32, (sub, sub), 1)
    tx = ti ^ si
    tril = jnp.where(si <= ti, 1.0, 0.0).astype(BF16)
    diag_keep = jnp.where(si <= ti, tx, 8) < 8

    def head_block(off, hh):
        cols = slice(hh * HG_DIM, (hh + 1) * HG_DIM)
        lb = lb_ref[:, cols]
        gh = gh_ref[:, cols]
        q = q_ref[pl.ds(off, sub), cols]
        fa = f_ref[pl.ds(off, sub), cols]
        v = v_ref[pl.ds(off, sub), cols].astype(BF16)
        ga = g_ref[pl.ds(off, sub), cols]

        f = lb + (1.0 - lb) * _sigmoid(fa)
        logf = jnp.log(jnp.clip(f, F_MIN, 1.0))
        kk = 1.0 - f

        l_hi = logf.astype(BF16)
        r1 = logf - l_hi.astype(F32)
        l_mid = r1.astype(BF16)
        l_lo = (r1 - l_mid.astype(F32)).astype(BF16)
        b = _dot(tril, l_hi) + _dot(tril, l_mid) + _dot(tril, l_lo)

        a = jnp.zeros((sub, sub), F32)
        h = sub // 2
        while h >= 8:
            n = sub // (2 * h)
            b3 = b.reshape(n, 2 * h, HG_DIM)
            e = jnp.exp(-jnp.abs(b3 - b3[:, h - 1:h, :])).reshape(sub, HG_DIM)
            second = (row & h) != 0
            qd = jnp.where(second, q * e, 0.0).astype(BF16)
            kd = jnp.where(second, 0.0, kk * e).astype(BF16)
            p = _dot_nt(qd, kd)
            a = a + (p if 2 * h == sub else jnp.where(tx < 2 * h, p, 0.0))
            h //= 2
        b3 = b.reshape(sub // 8, 8, HG_DIM)
        xd = (b3 - b3[:, 3:4, :]).reshape(sub, HG_DIM)
        qd = (q * jnp.exp(xd)).astype(BF16)
        kd = (kk * jnp.exp(-xd)).astype(BF16)
        a = a + jnp.where(diag_keep, _dot_nt(qd, kd), 0.0)

        st = st_ref[hh]
        qi = (q * jnp.exp(b)).astype(BF16)
        o = _dot_nt(qi, st.astype(BF16)) + _dot(a.astype(BF16), v)
        b_last = b[sub - 1:sub, :]
        kdi = (kk * jnp.exp(b_last - b)).astype(BF16)
        st_ref[hh] = st * jnp.exp(b_last) + _dot_tn(v, kdi)

        ms = jnp.mean(o * o, axis=-1, keepdims=True)
        y = o * lax.rsqrt(ms + 1e-6) * gh * (ga * _sigmoid(ga))
        y_ref[pl.ds(off, sub), cols] = y.astype(BF16)

    def body(r, carry):
        off = pl.multiple_of(r * sub, sub)
        for hh in range(HG_GROUP):
            head_block(off, hh)
        return carry

    lax.fori_loop(0, nsub, body, 0)


def _hgrn(proj, lb, g_hg, batch, seq, lblk):
    t = proj.shape[0]
    nl = seq // lblk
    ng = HG_HEADS // HG_GROUP
    gw = HG_GROUP * HG_DIM
    col = lambda c0: (lambda b, h, l: (b * nl + l, c0 + h))
    blk = lambda c0: pl.BlockSpec((lblk, gw), col(c0))
    vec = pl.BlockSpec((None, 1, gw), lambda b, h, l: (h, 0, 0))
    return pl.pallas_call(
        functools.partial(_hgrn_kernel, nsub=lblk // HGRN_SUB),
        grid=(batch, ng, nl),
        in_specs=[blk(0), blk(ng), blk(2 * ng), blk(3 * ng), vec, vec],
        out_specs=pl.BlockSpec((lblk, gw), col(0)),
        out_shape=jax.ShapeDtypeStruct((t, HG_WIDTH), BF16),
        scratch_shapes=[pltpu.VMEM((HG_GROUP, HG_DIM, HG_DIM), F32)],
        compiler_params=_params("arbitrary", "arbitrary", "arbitrary"),
        name="hgrn2",
    )(proj, proj, proj, proj, lb.reshape(ng, 1, gw), g_hg.reshape(ng, 1, gw))


def _wt_kernel(w_ref, o_ref, *, scale):
    o_ref[...] = (w_ref[...].T * scale).astype(BF16)


def _transpose_weights(w, scale):
    depth, r, n = w.shape
    tn = _largest_tile(n, 512, 128)
    return pl.pallas_call(
        functools.partial(_wt_kernel, scale=scale),
        grid=(depth, n // tn),
        in_specs=[pl.BlockSpec((None, r, tn), lambda l, j: (l, 0, j))],
        out_specs=pl.BlockSpec((None, tn, r), lambda l, j: (l, j, 0)),
        out_shape=jax.ShapeDtypeStruct((depth, n, r), BF16),
        compiler_params=_params("arbitrary", "arbitrary"),
        name="w_transpose",
    )(w)


def _dsa_prep_kernel(cq_ref, ckv_ref, ki_ref, gcq_ref, gckv_ref, wqt_ref, wit_ref,
                     qt_ref, qit_ref, kidx_ref, ckvo_ref, ckvt_ref, wt_o_ref):
    cq = cq_ref[...]
    cqn = cq * lax.rsqrt(jnp.mean(cq * cq, axis=-1, keepdims=True) + 1e-6) * gcq_ref[...]
    cqn_t = cqn.T.astype(BF16)
    step = 512
    for c0 in range(0, SA_HEADS * SA_LATENT, step):
        qt_ref[c0:c0 + step, :] = _dot(wqt_ref[c0:c0 + step, :], cqn_t).astype(BF16)
    for c0 in range(0, IDX_HEADS * IDX_DIM, step):
        qit_ref[c0:c0 + step, :] = _dot(wit_ref[c0:c0 + step, :], cqn_t).astype(BF16)

    ckv = ckv_ref[...]
    ckvn = ckv * lax.rsqrt(jnp.mean(ckv * ckv, axis=-1, keepdims=True) + 1e-6) * gckv_ref[...]
    ckvo_ref[...] = ckvn.astype(BF16)
    ckvt_ref[0:SA_LATENT, :] = ckvn.T.astype(BF16)
    ckvt_ref[SA_LATENT:KVT_ROWS, :] = jnp.ones((KVT_ROWS - SA_LATENT, ckvt_ref.shape[1]), BF16)

    ki = ki_ref[...]
    kidx_ref[...] = ki.astype(BF16)
    ki_t = ki.T
    wt_o_ref[...] = ki_t[IDX_DIM:IDX_DIM + IDX_HEADS, :] * (IDX_HEADS ** -0.5 * IDX_DIM ** -0.5)


def _dsa_prep(proj, g_cq, g_ckv, w_uq_t, w_iq_t, layer, batch, seq, col_cq, tm):
    nq = SA_HEADS * SA_LATENT
    ni = IDX_HEADS * IDX_DIM
    per_b = seq // tm
    cq_blk = col_cq // SA_Q_RANK
    ckv_blk = (col_cq + SA_Q_RANK) // SA_LATENT
    ki_blk = (col_cq + SA_Q_RANK + SA_LATENT) // KI_WIDTH
    tok = lambda i: (i // per_b, i % per_b)
    return pl.pallas_call(
        _dsa_prep_kernel,
        grid=(batch * per_b,),
        in_specs=[
            pl.BlockSpec((tm, SA_Q_RANK), lambda i: (i, cq_blk)),
            pl.BlockSpec((tm, SA_LATENT), lambda i: (i, ckv_blk)),
            pl.BlockSpec((tm, KI_WIDTH), lambda i: (i, ki_blk)),
            pl.BlockSpec((1, SA_Q_RANK), lambda i: (0, 0)),
            pl.BlockSpec((1, SA_LATENT), lambda i: (0, 0)),
            pl.BlockSpec((None, nq, SA_Q_RANK), lambda i: (layer, 0, 0)),
            pl.BlockSpec((None, ni, SA_Q_RANK), lambda i: (layer, 0, 0)),
        ],
        out_specs=[
            pl.BlockSpec((None, nq, tm), lambda i: (tok(i)[0], 0, tok(i)[1])),
            pl.BlockSpec((None, ni, tm), lambda i: (tok(i)[0], 0, tok(i)[1])),
            pl.BlockSpec((None, tm, KI_WIDTH), lambda i: (tok(i)[0], tok(i)[1], 0)),
            pl.BlockSpec((None, tm, SA_LATENT), lambda i: (tok(i)[0], tok(i)[1], 0)),
            pl.BlockSpec((None, None, KVT_ROWS, tm), lambda i: (tok(i)[0], tok(i)[1], 0, 0)),
            pl.BlockSpec((None, IDX_HEADS, tm), lambda i: (tok(i)[0], 0, tok(i)[1])),
        ],
        out_shape=[
            jax.ShapeDtypeStruct((batch, nq, seq), BF16),
            jax.ShapeDtypeStruct((batch, ni, seq), BF16),
            jax.ShapeDtypeStruct((batch, seq, KI_WIDTH), BF16),
            jax.ShapeDtypeStruct((batch, seq, SA_LATENT), BF16),
            jax.ShapeDtypeStruct((batch, per_b, KVT_ROWS, tm), BF16),
            jax.ShapeDtypeStruct((batch, IDX_HEADS, seq), F32),
        ],
        compiler_params=_params("arbitrary"),
        name="dsa_prep",
    )(proj, proj, proj, g_cq.reshape(1, -1), g_ckv.reshape(1, -1), w_uq_t, w_iq_t)


def _attn_kernel(qt_ref, qit_ref, wt_ref, kidx_ref, ckv_ref, ckvt_ref, wuvt_ref, y_ref,
                 qcat, qicat, keys, half, thr_scr, pbuf, acc, m_scr, a_scr, rise_scr, ybuf, *, n_sel, slopes):
    qb = pl.program_id(1)
    lanes = Q_BLOCK
    hl = SA_HEADS * lanes

    for h in range(SA_HEADS):
        qcat[:, h * lanes:(h + 1) * lanes] = qt_ref[h * SA_LATENT:(h + 1) * SA_LATENT, :]
        qicat[0:IDX_DIM, h * lanes:(h + 1) * lanes] = qit_ref[h * IDX_DIM:(h + 1) * IDX_DIM, :]
    qicat[IDX_DIM:KI_WIDTH, :] = jnp.zeros((KI_WIDTH - IDX_DIM, hl), BF16)

    per_tile = ATT_TK // Q_BLOCK
    n_full = (qb + 1) // per_tile
    rem = (qb + 1) % per_tile
    n_att = n_full + jnp.minimum(rem, 1)
    q_chunk = jnp.right_shift(qb * Q_BLOCK + lax.broadcasted_iota(jnp.int32, (1, lanes), 1), CHUNK_SHIFT)
    i16_min = -(2 ** 15)

    def idx_tile(j, rows):
        off = pl.multiple_of(j * ATT_TK, ATT_TK)
        kt = kidx_ref[pl.ds(off, rows), :]
        sc = jnp.zeros((rows, lanes), F32)
        for hp in range(IDX_HEADS // 2):
            z = _dot(kt, qicat[:, hp * 2 * lanes:(hp + 1) * 2 * lanes])
            for u in range(2):
                h = 2 * hp + u
                sc = sc + wt_ref[h:h + 1, :] * jnp.maximum(z[:, u * lanes:(u + 1) * lanes], 0.0)
        k_chunk = jnp.right_shift(off + lax.broadcasted_iota(jnp.int32, (rows, lanes), 0), CHUNK_SHIFT)
        bits = lax.bitcast_convert_type(sc, jnp.int32)
        key = jnp.where(bits < 0, bits ^ 0x7FFFFFFF, bits)
        key = jnp.where(k_chunk <= q_chunk, key, INT_MIN)
        keys[pl.ds(off, rows), :] = key
        half[pl.ds(off, rows), :] = jnp.right_shift(key, 16).astype(jnp.int16)
        if rows < ATT_TK:
            keys[pl.ds(off + rows, ATT_TK - rows), :] = jnp.full((ATT_TK - rows, lanes), INT_MIN, jnp.int32)
            half[pl.ds(off + rows, ATT_TK - rows), :] = jnp.full((ATT_TK - rows, lanes), i16_min, jnp.int16)

    def idx_body(j, carry):
        idx_tile(j, ATT_TK)
        return carry

    lax.fori_loop(0, n_full, idx_body, 0)
    for v in range(1, per_tile):
        pl.when(rem == v)(functools.partial(idx_tile, n_full, v * Q_BLOCK))


    def tree_sum(parts):
        while len(parts) > 1:
            parts = [a + b for a, b in zip(parts[0::2], parts[1::2])] + parts[len(parts) & ~1:]
        return parts[0]

    def threshold_search(n_tiles):
        n_rows = n_tiles * ATT_TK

        def search16(need):
            def bit_body(it, t):
                cand = t + jnp.left_shift(jnp.int32(1), 15 - it)
                cand16 = jnp.broadcast_to(cand, (16, lanes)).astype(jnp.int16)
                parts = [jnp.zeros((16, lanes), jnp.int16)] * COUNT_CHAINS
                for i, r0 in enumerate(range(0, n_rows, 16)):
                    ge = jnp.where(half[r0:r0 + 16, :] >= cand16, jnp.int16(1), jnp.int16(0))
                    parts[i % COUNT_CHAINS] = parts[i % COUNT_CHAINS] + ge
                cnt = jnp.sum(tree_sum(parts).astype(jnp.int32), axis=0, keepdims=True)
                return jnp.where(cnt >= need, cand, t)

            return lax.fori_loop(0, 16, bit_body, jnp.full((1, lanes), i16_min, jnp.int32))

        t_hi = search16(jnp.full((1, lanes), n_sel, jnp.int32))
        above = []
        for r0 in range(0, n_rows, 64):
            k = keys[r0:r0 + 64, :]
            hi = jnp.right_shift(k, 16)
            lo = (k & 0xFFFF) + i16_min
            half[r0:r0 + 64, :] = jnp.where(hi == t_hi, lo, i16_min).astype(jnp.int16)
            above.append(jnp.sum(jnp.where(hi > t_hi, 1, 0).reshape(8, 8, lanes), axis=0))
        t_lo = search16(n_sel - jnp.sum(tree_sum(above), axis=0, keepdims=True))
        thr = jnp.left_shift(t_hi, 16) + (t_lo - i16_min)
        thr_scr[...] = jnp.maximum(thr, INT_MIN + 1)

    for n_tiles in range(1, keys.shape[0] // ATT_TK + 1):
        pl.when(n_att == n_tiles)(functools.partial(threshold_search, n_tiles))
    thr = thr_scr[...]

    def scores(j, rows):
        off = pl.multiple_of(j * ATT_TK, ATT_TK)
        kv = ckv_ref[pl.ds(off, rows), :]
        q_pos = qb * Q_BLOCK + lax.broadcasted_iota(jnp.int32, (rows, lanes), 1)
        k_pos = off + lax.broadcasted_iota(jnp.int32, (rows, lanes), 0)
        dist = jnp.where(keys[pl.ds(off, rows), :] >= thr, jnp.abs(q_pos - k_pos).astype(F32), MASKED_DIST)
        for hp in range(SA_HEADS // 2):
            s2 = _dot(kv, qcat[:, hp * 2 * lanes:(hp + 1) * 2 * lanes])
            for u in range(2):
                h = 2 * hp + u
                yield h * lanes, s2[:, u * lanes:(u + 1) * lanes] - slopes[h] * dist

    def reset():
        m_scr[...] = jnp.full((1, hl), M_INIT, F32)
        acc[...] = jnp.zeros((KVT_ROWS, hl), F32)

    def safe_tile(j, rows=ATT_TK):
        for lo, s in scores(j, rows):
            m_old = m_scr[:, lo:lo + lanes]
            m_new = jnp.maximum(m_old, jnp.max(s, axis=0, keepdims=True))
            a_scr[:, lo:lo + lanes] = jnp.exp2(m_old - m_new)
            m_scr[:, lo:lo + lanes] = m_new
            pbuf[0:rows, lo:lo + lanes] = jnp.exp2(s - m_new).astype(BF16)
        kvt = ckvt_ref[j, :, 0:rows]
        for hp in range(SA_HEADS // 2):
            cs = slice(hp * 2 * lanes, (hp + 1) * 2 * lanes)
            acc[:, cs] = acc[:, cs] * a_scr[:, cs] + _dot(kvt, pbuf[0:rows, cs])

    def fast_tile(j):
        for lo, s in scores(j, ATT_TK):
            ref = m_scr[:, lo:lo + lanes]
            pbuf[:, lo:lo + lanes] = jnp.exp2(s - ref).astype(BF16)
            m_tile = jnp.max(s, axis=0, keepdims=True)
            m_new = jnp.maximum(ref, m_tile)
            a_scr[:, lo:lo + lanes] = jnp.exp2(ref - m_new)
            m_scr[:, lo:lo + lanes] = m_new
            rise_scr[:, lo:lo + lanes] = jnp.maximum(rise_scr[:, lo:lo + lanes], m_tile - ref)
        kvt = ckvt_ref[j]
        for hp in range(SA_HEADS // 2):
            cs = slice(hp * 2 * lanes, (hp + 1) * 2 * lanes)
            acc[:, cs] = (acc[:, cs] + _dot(kvt, pbuf[:, cs])) * a_scr[:, cs]

    reset()
    rise_scr[...] = jnp.zeros((1, hl), F32)
    for v in range(1, per_tile):
        pl.when(rem == v)(functools.partial(safe_tile, n_full, v * Q_BLOCK))
    n_two_pass = jnp.minimum(n_full, 2 - jnp.minimum(rem, 1))

    def two_pass_body(i, carry):
        safe_tile(n_full - 1 - i)
        return carry

    def one_pass_body(i, carry):
        fast_tile(n_full - 1 - n_two_pass - i)
        return carry

    lax.fori_loop(0, n_two_pass, two_pass_body, 0)
    lax.fori_loop(0, n_full - n_two_pass, one_pass_body, 0)

    @pl.when(jnp.max(rise_scr[...]) > RESCALE_GUARD)
    def _():
        reset()

        def redo_body(j, carry):
            safe_tile(j)
            return carry

        lax.fori_loop(0, n_att, redo_body, 0)

    for h in range(SA_HEADS):
        lo = h * lanes
        o = acc[0:SA_LATENT, lo:lo + lanes] / acc[SA_LATENT:SA_LATENT + 1, lo:lo + lanes]
        ybuf[h * SA_V_DIM:(h + 1) * SA_V_DIM, :] = _dot(wuvt_ref[h], o.astype(BF16))
    y_ref[...] = ybuf[...].T.astype(BF16)


def _attn(qt, qit, wt, kidx, ckv, ckvt, w_uvt, layer, batch, seq):
    nq = SA_HEADS * SA_LATENT
    ni = IDX_HEADS * IDX_DIM
    nblk = seq // Q_BLOCK
    hl = SA_HEADS * Q_BLOCK
    n_sel = min(TOPK_MAX, seq // 4)
    slopes = tuple(float(v) * LOG2E
                   for v in np.exp2(-8.0 * (np.arange(SA_HEADS, dtype=np.float64) + 1.0) / SA_HEADS))
    assert min(slopes) * MASKED_DIST > -2.0 * M_INIT
    return pl.pallas_call(
        functools.partial(_attn_kernel, n_sel=n_sel, slopes=slopes),
        grid=(batch, nblk),
        in_specs=[
            pl.BlockSpec((None, nq, Q_BLOCK), lambda b, i: (b, 0, i)),
            pl.BlockSpec((None, ni, Q_BLOCK), lambda b, i: (b, 0, i)),
            pl.BlockSpec((None, IDX_HEADS, Q_BLOCK), lambda b, i: (b, 0, i)),
            pl.BlockSpec((None, seq, KI_WIDTH), lambda b, i: (b, 0, 0)),
            pl.BlockSpec((None, seq, SA_LATENT), lambda b, i: (b, 0, 0)),
            pl.BlockSpec((None, seq // ATT_TK, KVT_ROWS, ATT_TK), lambda b, i: (b, 0, 0, 0)),
            pl.BlockSpec((None, SA_HEADS, SA_V_DIM, SA_LATENT), lambda b, i: (layer, 0, 0, 0)),
        ],
        out_specs=pl.BlockSpec((Q_BLOCK, SA_WIDTH), lambda b, i: (b * nblk + i, 0)),
        out_shape=jax.ShapeDtypeStruct((batch * seq, SA_WIDTH), BF16),
        scratch_shapes=[
            pltpu.VMEM((SA_LATENT, hl), BF16),
            pltpu.VMEM((KI_WIDTH, hl), BF16),
            pltpu.VMEM((seq, Q_BLOCK), jnp.int32),
            pltpu.VMEM((seq, Q_BLOCK), jnp.int16),
            pltpu.VMEM((1, Q_BLOCK), jnp.int32),
            pltpu.VMEM((ATT_TK, hl), BF16),
            pltpu.VMEM((KVT_ROWS, hl), F32),
            pltpu.VMEM((1, hl), F32),
            pltpu.VMEM((1, hl), F32),
            pltpu.VMEM((1, hl), F32),
            pltpu.VMEM((SA_WIDTH, Q_BLOCK), F32),
        ],
        compiler_params=_params("arbitrary", "arbitrary"),
        name="dsa_attn",
    )(qt, qit, wt, kidx, ckv, ckvt, w_uvt)


def _merge_kernel(ya_ref, yb_ref, ga_ref, gb_ref, x_ref, gt_ref, wpa_ref, wpb_ref, wout_ref,
                  lng_ref, lnb_ref, o_ref, mbuf, zbuf, *, alpha, tc):
    d = x_ref.shape[1]
    ya = ya_ref[...]
    yb = yb_ref[...]
    for c0 in range(0, d, tc):
        cs = slice(c0, c0 + tc)
        m = (_sigmoid(ga_ref[:, cs]) * _dot(ya, wpa_ref[:, cs])
             + _sigmoid(gb_ref[:, cs]) * _dot(yb, wpb_ref[:, cs]))
        mbuf[:, cs] = m.astype(BF16)
    mm = mbuf[...]
    for c0 in range(0, d, tc):
        cs = slice(c0, c0 + tc)
        zbuf[:, cs] = alpha * x_ref[:, cs] + (1.0 + gt_ref[:, cs]) * _dot(mm, wout_ref[:, cs])
    o_ref[...] = _layer_norm(zbuf[...], lng_ref[...], lnb_ref[...])


def _merge(ya, yb, proj, x2d, gt, wpa, wpb, wout, lng, lnb, layer, seq, col_ga, alpha, tm):
    t, d = x2d.shape
    per_b = seq // tm
    ga_blk = col_ga // d
    const = lambda i: (0, 0)
    wsel = lambda i: (layer, 0, 0)
    return pl.pallas_call(
        functools.partial(_merge_kernel, alpha=alpha, tc=min(d, 512)),
        grid=(t // tm,),
        in_specs=[
            pl.BlockSpec((tm, HG_WIDTH), lambda i: (i, 0)),
            pl.BlockSpec((tm, SA_WIDTH), lambda i: (i, 0)),
            pl.BlockSpec((tm, d), lambda i: (i, ga_blk)),
            pl.BlockSpec((tm, d), lambda i: (i, ga_blk + 1)),
            pl.BlockSpec((tm, d), lambda i: (i, 0)),
            pl.BlockSpec((None, 1, d), lambda i: (i // per_b, 0, 0)),
            pl.BlockSpec((None, HG_WIDTH, d), wsel, pipeline_mode=pl.Buffered(1)),
            pl.BlockSpec((None, SA_WIDTH, d), wsel, pipeline_mode=pl.Buffered(1)),
            pl.BlockSpec((None, d, d), wsel, pipeline_mode=pl.Buffered(1)),
            pl.BlockSpec((1, d), const),
            pl.BlockSpec((1, d), const),
        ],
        out_specs=pl.BlockSpec((tm, d), lambda i: (i, 0)),
        out_shape=jax.ShapeDtypeStruct((t, d), F32),
        scratch_shapes=[pltpu.VMEM((tm, d), BF16), pltpu.VMEM((tm, d), F32)],
        compiler_params=_params("arbitrary"),
        name="merge_out",
    )(ya, yb, proj, proj, x2d, gt, wpa, wpb, wout, lng.reshape(1, d), lnb.reshape(1, d))


def _swiglu_kernel(x_ref, sc_ref, sh_ref, gt_ref, wg_ref, wu_ref, wd_ref, lng_ref, lnb_ref,
                   o_ref, hbuf, acc, *, alpha):
    f = pl.program_id(1)

    @pl.when(f == 0)
    def _():
        hbuf[...] = (x_ref[...] * (1.0 + sc_ref[...]) + sh_ref[...]).astype(BF16)
        acc[...] = jnp.zeros_like(acc)

    h = hbuf[...]
    g = _dot(h, wg_ref[...])
    u = _dot(h, wu_ref[...])
    a = (g * _sigmoid(g) * u).astype(BF16)
    acc[...] += _dot(a, wd_ref[...])

    @pl.when(f == pl.num_programs(1) - 1)
    def _():
        z = alpha * x_ref[...] + (1.0 + gt_ref[...]) * acc[...]
        o_ref[...] = _layer_norm(z, lng_ref[...], lnb_ref[...])


def _swiglu(x2d, sc, sh, gt, wg, wu, wd, lng, lnb, layer, seq, alpha, tm, tf):
    t, d = x2d.shape
    dff = wg.shape[2]
    per_b = seq // tm
    mod = pl.BlockSpec((None, 1, d), lambda i, f: (i // per_b, 0, 0))
    const = pl.BlockSpec((1, d), lambda i, f: (0, 0))
    return pl.pallas_call(
        functools.partial(_swiglu_kernel, alpha=alpha),
        grid=(t // tm, dff // tf),
        in_specs=[
            pl.BlockSpec((tm, d), lambda i, f: (i, 0)),
            mod, mod, mod,
            pl.BlockSpec((None, d, tf), lambda i, f: (layer, 0, f)),
            pl.BlockSpec((None, d, tf), lambda i, f: (layer, 0, f)),
            pl.BlockSpec((None, tf, d), lambda i, f: (layer, f, 0)),
            const, const,
        ],
        out_specs=pl.BlockSpec((tm, d), lambda i, f: (i, 0)),
        out_shape=jax.ShapeDtypeStruct((t, d), F32),
        scratch_shapes=[pltpu.VMEM((tm, d), BF16), pltpu.VMEM((tm, d), F32)],
        compiler_params=_params("arbitrary", "arbitrary"),
        name="swiglu",
    )(x2d, sc, sh, gt, wg, wu, wd, lng.reshape(1, d), lnb.reshape(1, d))


def _regroup_kernel(w_ref, o_ref, *, groups):
    row = 0
    for lo, hi in groups:
        o_ref[row:row + hi - lo, :] = w_ref[lo:hi, :].astype(BF16)
        row += hi - lo
    o_ref[row:, :] = jnp.zeros((o_ref.shape[0] - row, o_ref.shape[1]), BF16)


def _regroup_w_in(w_in_t, n_proj):
    depth, d_in, d = w_in_t.shape
    small = SA_Q_RANK + SA_LATENT + IDX_DIM + IDX_HEADS
    hg4 = 4 * HG_WIDTH
    groups = ((0, hg4), (hg4 + small, d_in), (hg4, hg4 + small))
    cols = 256
    return pl.pallas_call(
        functools.partial(_regroup_kernel, groups=groups),
        grid=(depth, d // cols),
        in_specs=[pl.BlockSpec((None, d_in, cols), lambda l, i: (l, 0, i))],
        out_specs=pl.BlockSpec((None, n_proj, cols), lambda l, i: (l, 0, i)),
        out_shape=jax.ShapeDtypeStruct((depth, n_proj, d), BF16),
        compiler_params=_params("arbitrary", "arbitrary"),
        name="w_in_regroup",
    )(w_in_t)


def _largest_tile(n, cap, quantum):
    best = quantum
    for cand in range(quantum, min(n, cap) + 1, quantum):
        if n % cand == 0:
            best = cand
    return best


def kernel(x, c, w_ada, b_ada, w_in, lb_logits, g_hg, w_pa, g_cq, g_ckv, w_uq, w_iq, w_uv, w_pb,
           w_out, ln1_g, ln1_b, w_gate, w_up, w_down, ln2_g, ln2_b):
    batch, seq, d = x.shape
    depth = w_ada.shape[0]
    dff = w_gate.shape[2]
    alpha = (2 * depth) ** 0.25
    assert seq % ATT_TK == 0 and d % 256 == 0 and batch <= ADA_PAD_ROWS

    hg4 = 4 * HG_WIDTH
    col_ga = hg4
    col_cq = hg4 + 2 * d
    used = col_cq + SA_Q_RANK + SA_LATENT + KI_WIDTH
    tn = 1536 if d >= 1024 else 512
    n_proj = -(-used // tn) * tn
    w_in_r = _regroup_w_in(jnp.swapaxes(w_in, 1, 2), n_proj)

    scale = SA_LATENT ** -0.5 * LOG2E
    w_uq_t = _transpose_weights(w_uq, scale)
    w_iq_t = _transpose_weights(w_iq, 1.0)
    w_uvt = jnp.swapaxes(w_uv, 2, 3).astype(BF16)
    w_pa_b, w_pb_b, w_out_b = w_pa.astype(BF16), w_pb.astype(BF16), w_out.astype(BF16)
    w_gate_b, w_up_b, w_down_b = w_gate.astype(BF16), w_up.astype(BF16), w_down.astype(BF16)

    p = jax.nn.softmax(lb_logits.astype(F32), axis=0)
    lbs = jnp.cumsum(p, axis=0) - p[0:1]

    c_pad = jnp.zeros((ADA_PAD_ROWS, d), F32).at[:batch].set(c)
    mods = _ada(c_pad, w_ada, b_ada)[:, :batch].reshape(depth, batch, 6, 1, d)

    tm_proj = _largest_tile(seq, 1024, 256)
    lblk = _largest_tile(seq, 1024, HGRN_SUB)
    tm_prep = ATT_TK
    tm_merge = 256
    tm_ffn = _largest_tile(seq, 512, 256)
    tf = _largest_tile(dff, 512, 128)

    x2d = x.reshape(batch * seq, d)
    for l in range(depth):
        sh_m, sc_m, gt_m, sh_f, sc_f, gt_f = (mods[l, :, k] for k in range(6))
        proj = _inproj(x2d, sc_m, sh_m, w_in_r, l, seq, tm_proj, tn)
        y_a = _hgrn(proj, lbs[l], g_hg[l], batch, seq, lblk)
        qt, qit, kidx, ckv, ckvt, wt = _dsa_prep(proj, g_cq[l], g_ckv[l], w_uq_t, w_iq_t, l, batch, seq, col_cq, tm_prep)
        y_b = _attn(qt, qit, wt, kidx, ckv, ckvt, w_uvt, l, batch, seq)
        x2d = _merge(y_a, y_b, proj, x2d, gt_m, w_pa_b, w_pb_b, w_out_b, ln1_g[l], ln1_b[l],
                     l, seq, col_ga, alpha, tm_merge)
        x2d = _swiglu(x2d, sc_f, sh_f, gt_f, w_gate_b, w_up_b, w_down_b, ln2_g[l], ln2_b[l],
                      l, seq, alpha, tm_ffn, tf)
    return x2d.reshape(batch, seq, d)
```

```python
import functools

import numpy as np
import jax
import jax.numpy as jnp
from jax import lax
from jax.experimental import pallas as pl
from jax.experimental.pallas import tpu as pltpu

F32 = jnp.float32
BF16 = jnp.bfloat16

HG_HEADS = 8
HG_DIM = 128
F_MIN = 1e-6
SA_HEADS = 16
SA_LATENT = 256
SA_Q_RANK = 512
SA_V_DIM = 64
IDX_HEADS = 16
IDX_DIM = 64
TOPK_MAX = 256
CHUNK = 64
CHUNK_SHIFT = 6
Q_BLOCK = 128
ADA_PAD_ROWS = 8

HG_WIDTH = HG_HEADS * HG_DIM
SA_WIDTH = SA_HEADS * SA_V_DIM
KI_WIDTH = 128

VMEM_LIMIT = 56 * 1024 * 1024
INT_MIN = -(2 ** 31)
MASKED_DIST = 1e32
M_INIT = -1e29

HGRN_SUB = 256
HG_GROUP = 8
ATT_TK = 512
SEARCH_ROWS = 256
COUNT_CHAINS = 8
RESCALE_GUARD = 100.0
KVT_ROWS = SA_LATENT + 16
LOG2E = 1.4426950408889634


def _dot(a, b):
    return jnp.dot(a, b, preferred_element_type=F32)


def _dot_nt(a, b):
    return lax.dot_general(a, b, (((1,), (1,)), ((), ())), preferred_element_type=F32)


def _dot_tn(a, b):
    return lax.dot_general(a, b, (((0,), (0,)), ((), ())), preferred_element_type=F32)


def _sigmoid(x):
    return 1.0 / (1.0 + jnp.exp(-x))


def _layer_norm(z, g, b):
    mu = jnp.mean(z, axis=-1, keepdims=True)
    zc = z - mu
    var = jnp.mean(zc * zc, axis=-1, keepdims=True)
    return zc * lax.rsqrt(var + 1e-5) * g + b


def _params(*sem):
    return pltpu.CompilerParams(dimension_semantics=sem, vmem_limit_bytes=VMEM_LIMIT)


def _ada_kernel(c_ref, w_ref, b_ref, o_ref):
    c = c_ref[...]
    cond = c * _sigmoid(c)
    w = w_ref[...]
    c_hi = cond.astype(BF16)
    c_lo = (cond - c_hi.astype(F32)).astype(BF16)
    w_hi = w.astype(BF16)
    w_lo = (w - w_hi.astype(F32)).astype(BF16)
    acc = _dot(c_hi, w_hi) + _dot(c_hi, w_lo) + _dot(c_lo, w_hi)
    o_ref[...] = acc + b_ref[...]


def _ada(c_pad, w_ada, b_ada):
    depth, d, n = w_ada.shape
    tn = _largest_tile(n, 1024, 128)
    return pl.pallas_call(
        _ada_kernel,
        grid=(depth, n // tn),
        in_specs=[
            pl.BlockSpec((ADA_PAD_ROWS, d), lambda l, j: (0, 0)),
            pl.BlockSpec((None, d, tn), lambda l, j: (l, 0, j)),
            pl.BlockSpec((None, 1, tn), lambda l, j: (l, 0, j)),
        ],
        out_specs=pl.BlockSpec((None, ADA_PAD_ROWS, tn), lambda l, j: (l, 0, j)),
        out_shape=jax.ShapeDtypeStruct((depth, ADA_PAD_ROWS, n), F32),
        compiler_params=_params("arbitrary", "arbitrary"),
        name="ada_mod",
    )(c_pad, w_ada, b_ada.reshape(depth, 1, n))


def _inproj_kernel(x_ref, sc_ref, sh_ref, w_ref, o_ref, h_ref):
    @pl.when(pl.program_id(1) == 0)
    def _():
        h_ref[...] = (x_ref[...] * (1.0 + sc_ref[...]) + sh_ref[...]).astype(BF16)

    o_ref[...] = _dot_nt(h_ref[...], w_ref[...])


def _inproj(x2d, sc, sh, w, layer, seq, tm, tn):
    t, d = x2d.shape
    n = w.shape[1]
    per_b = seq // tm
    return pl.pallas_call(
        _inproj_kernel,
        grid=(t // tm, n // tn),
        in_specs=[
            pl.BlockSpec((tm, d), lambda i, j: (i, 0)),
            pl.BlockSpec((None, 1, d), lambda i, j: (i // per_b, 0, 0)),
            pl.BlockSpec((None, 1, d), lambda i, j: (i // per_b, 0, 0)),
            pl.BlockSpec((None, tn, d), lambda i, j: (layer, j, 0)),
        ],
        out_specs=pl.BlockSpec((tm, tn), lambda i, j: (i, j)),
        out_shape=jax.ShapeDtypeStruct((t, n), F32),
        scratch_shapes=[pltpu.VMEM((tm, d), BF16)],
        compiler_params=_params("arbitrary", "arbitrary"),
        name="in_proj",
    )(x2d, sc, sh, w)


def _hgrn_kernel(q_ref, f_ref, v_ref, g_ref, lb_ref, gh_ref, y_ref, st_ref, *, nsub):
    sub = HGRN_SUB

    @pl.when(pl.program_id(2) == 0)
    def _():
        st_ref[...] = jnp.zeros_like(st_ref)

    row = lax.broadcasted_iota(jnp.int32, (sub, HG_DIM), 0)
    ti = lax.broadcasted_iota(jnp.int32, (sub, sub), 0)
    si = lax.broadcasted_iota(jnp.int32, (sub, sub), 1)
    tx = ti ^ si
    tril = jnp.where(si <= ti, 1.0, 0.0).astype(BF16)
    diag_keep = jnp.where(si <= ti, tx, 8) < 8

    def head_block(off, hh):
        cols = slice(hh * HG_DIM, (hh + 1) * HG_DIM)
        lb = lb_ref[:, cols]
        gh = gh_ref[:, cols]
        q = q_ref[pl.ds(off, sub), cols]
        fa = f_ref[pl.ds(off, sub), cols]
        v = v_ref[pl.ds(off, sub), cols].astype(BF16)
        ga = g_ref[pl.ds(off, sub), cols]

        f = lb + (1.0 - lb) * _sigmoid(fa)
        logf = jnp.log(jnp.clip(f, F_MIN, 1.0))
        kk = 1.0 - f

        l_hi = logf.astype(BF16)
        r1 = logf - l_hi.astype(F32)
        l_mid = r1.astype(BF16)
        l_lo = (r1 - l_mid.astype(F32)).astype(BF16)
        b = _dot(tril, l_hi) + _dot(tril, l_mid) + _dot(tril, l_lo)

        a = jnp.zeros((sub, sub), F32)
        h = sub // 2
        while h >= 8:
            n = sub // (2 * h)
            b3 = b.reshape(n, 2 * h, HG_DIM)
            e = jnp.exp(-jnp.abs(b3 - b3[:, h - 1:h, :])).reshape(sub, HG_DIM)
            second = (row & h) != 0
            qd = jnp.where(second, q * e, 0.0).astype(BF16)
            kd = jnp.where(second, 0.0, kk * e).astype(BF16)
            p = _dot_nt(qd, kd)
            a = a + (p if 2 * h == sub else jnp.where(tx < 2 * h, p, 0.0))
            h //= 2
        b3 = b.reshape(sub // 8, 8, HG_DIM)
        xd = (b3 - b3[:, 3:4, :]).reshape(sub, HG_DIM)
        qd = (q * jnp.exp(xd)).astype(BF16)
        kd = (kk * jnp.exp(-xd)).astype(BF16)
        a = a + jnp.where(diag_keep, _dot_nt(qd, kd), 0.0)

        st = st_ref[hh]
        qi = (q * jnp.exp(b)).astype(BF16)
        o = _dot_nt(qi, st.astype(BF16)) + _dot(a.astype(BF16), v)
        b_last = b[sub - 1:sub, :]
        kdi = (kk * jnp.exp(b_last - b)).astype(BF16)
        st_ref[hh] = st * jnp.exp(b_last) + _dot_tn(v, kdi)

        ms = jnp.mean(o * o, axis=-1, keepdims=True)
        y = o * lax.rsqrt(ms + 1e-6) * gh * (ga * _sigmoid(ga))
        y_ref[pl.ds(off, sub), cols] = y.astype(BF16)

    def body(r, carry):
        off = pl.multiple_of(r * sub, sub)
        for hh in range(HG_GROUP):
            head_block(off, hh)
        return carry

    lax.fori_loop(0, nsub, body, 0)


def _hgrn(proj, lb, g_hg, batch, seq, lblk):
    t = proj.shape[0]
    nl = seq // lblk
    ng = HG_HEADS // HG_GROUP
    gw = HG_GROUP * HG_DIM
    col = lambda c0: (lambda b, h, l: (b * nl + l, c0 + h))
    blk = lambda c0: pl.BlockSpec((lblk, gw), col(c0))
    vec = pl.BlockSpec((None, 1, gw), lambda b, h, l: (h, 0, 0))
    return pl.pallas_call(
        functools.partial(_hgrn_kernel, nsub=lblk // HGRN_SUB),
        grid=(batch, ng, nl),
        in_specs=[blk(0), blk(ng), blk(2 * ng), blk(3 * ng), vec, vec],
        out_specs=pl.BlockSpec((lblk, gw), col(0)),
        out_shape=jax.ShapeDtypeStruct((t, HG_WIDTH), BF16),
        scratch_shapes=[pltpu.VMEM((HG_GROUP, HG_DIM, HG_DIM), F32)],
        compiler_params=_params("arbitrary", "arbitrary", "arbitrary"),
        name="hgrn2",
    )(proj, proj, proj, proj, lb.reshape(ng, 1, gw), g_hg.reshape(ng, 1, gw))


def _wt_kernel(w_ref, o_ref, *, scale):
    o_ref[...] = (w_ref[...].T * scale).astype(BF16)


def _transpose_weights(w, scale):
    depth, r, n = w.shape
    tn = _largest_tile(n, 512, 128)
    return pl.pallas_call(
        functools.partial(_wt_kernel, scale=scale),
        grid=(depth, n // tn),
        in_specs=[pl.BlockSpec((None, r, tn), lambda l, j: (l, 0, j))],
        out_specs=pl.BlockSpec((None, tn, r), lambda l, j: (l, j, 0)),
        out_shape=jax.ShapeDtypeStruct((depth, n, r), BF16),
        compiler_params=_params("arbitrary", "arbitrary"),
        name="w_transpose",
    )(w)


def _dsa_prep_kernel(cq_ref, ckv_ref, ki_ref, gcq_ref, gckv_ref, wqt_ref, wit_ref,
                     qt_ref, qit_ref, kidx_ref, ckvo_ref, ckvt_ref, wt_o_ref):
    cq = cq_ref[...]
    cqn = cq * lax.rsqrt(jnp.mean(cq * cq, axis=-1, keepdims=True) + 1e-6) * gcq_ref[...]
    cqn_t = cqn.T.astype(BF16)
    step = 512
    for c0 in range(0, SA_HEADS * SA_LATENT, step):
        qt_ref[c0:c0 + step, :] = _dot(wqt_ref[c0:c0 + step, :], cqn_t).astype(BF16)
    for c0 in range(0, IDX_HEADS * IDX_DIM, step):
        qit_ref[c0:c0 + step, :] = _dot(wit_ref[c0:c0 + step, :], cqn_t).astype(BF16)

    ckv = ckv_ref[...]
    ckvn = ckv * lax.rsqrt(jnp.mean(ckv * ckv, axis=-1, keepdims=True) + 1e-6) * gckv_ref[...]
    ckvo_ref[...] = ckvn.astype(BF16)
    ckvt_ref[0:SA_LATENT, :] = ckvn.T.astype(BF16)
    ckvt_ref[SA_LATENT:KVT_ROWS, :] = jnp.ones((KVT_ROWS - SA_LATENT, ckvt_ref.shape[1]), BF16)

    ki = ki_ref[...]
    kidx_ref[...] = ki.astype(BF16)
    ki_t = ki.T
    wt_o_ref[...] = ki_t[IDX_DIM:IDX_DIM + IDX_HEADS, :] * (IDX_HEADS ** -0.5 * IDX_DIM ** -0.5)


def _dsa_prep(proj, g_cq, g_ckv, w_uq_t, w_iq_t, layer, batch, seq, col_cq, tm):
    nq = SA_HEADS * SA_LATENT
    ni = IDX_HEADS * IDX_DIM
    per_b = seq // tm
    cq_blk = col_cq // SA_Q_RANK
    ckv_blk = (col_cq + SA_Q_RANK) // SA_LATENT
    ki_blk = (col_cq + SA_Q_RANK + SA_LATENT) // KI_WIDTH
    tok = lambda i: (i // per_b, i % per_b)
    return pl.pallas_call(
        _dsa_prep_kernel,
        grid=(batch * per_b,),
        in_specs=[
            pl.BlockSpec((tm, SA_Q_RANK), lambda i: (i, cq_blk)),
            pl.BlockSpec((tm, SA_LATENT), lambda i: (i, ckv_blk)),
            pl.BlockSpec((tm, KI_WIDTH), lambda i: (i, ki_blk)),
            pl.BlockSpec((1, SA_Q_RANK), lambda i: (0, 0)),
            pl.BlockSpec((1, SA_LATENT), lambda i: (0, 0)),
            pl.BlockSpec((None, nq, SA_Q_RANK), lambda i: (layer, 0, 0)),
            pl.BlockSpec((None, ni, SA_Q_RANK), lambda i: (layer, 0, 0)),
        ],
        out_specs=[
            pl.BlockSpec((None, nq, tm), lambda i: (tok(i)[0], 0, tok(i)[1])),
            pl.BlockSpec((None, ni, tm), lambda i: (tok(i)[0], 0, tok(i)[1])),
            pl.BlockSpec((None, tm, KI_WIDTH), lambda i: (tok(i)[0], tok(i)[1], 0)),
            pl.BlockSpec((None, tm, SA_LATENT), lambda i: (tok(i)[0], tok(i)[1], 0)),
            pl.BlockSpec((None, None, KVT_ROWS, tm), lambda i: (tok(i)[0], tok(i)[1], 0, 0)),
            pl.BlockSpec((None, IDX_HEADS, tm), lambda i: (tok(i)[0], 0, tok(i)[1])),
        ],
        out_shape=[
            jax.ShapeDtypeStruct((batch, nq, seq), BF16),
            jax.ShapeDtypeStruct((batch, ni, seq), BF16),
            jax.ShapeDtypeStruct((batch, seq, KI_WIDTH), BF16),
            jax.ShapeDtypeStruct((batch, seq, SA_LATENT), BF16),
            jax.ShapeDtypeStruct((batch, per_b, KVT_ROWS, tm), BF16),
            jax.ShapeDtypeStruct((batch, IDX_HEADS, seq), F32),
        ],
        compiler_params=_params("arbitrary"),
        name="dsa_prep",
    )(proj, proj, proj, g_cq.reshape(1, -1), g_ckv.reshape(1, -1), w_uq_t, w_iq_t)


def _attn_kernel(qt_ref, qit_ref, wt_ref, kidx_ref, ckv_ref, ckvt_ref, wuvt_ref, y_ref,
                 qcat, qicat, keys, half, thr_scr, pbuf, acc, m_scr, a_scr, rise_scr, ybuf, *, n_sel, slopes):
    qb = pl.program_id(1)
    lanes = Q_BLOCK
    hl = SA_HEADS * lanes

    for h in range(SA_HEADS):
        qcat[:, h * lanes:(h + 1) * lanes] = qt_ref[h * SA_LATENT:(h + 1) * SA_LATENT, :]
        qicat[0:IDX_DIM, h * lanes:(h + 1) * lanes] = qit_ref[h * IDX_DIM:(h + 1) * IDX_DIM, :]
    qicat[IDX_DIM:KI_WIDTH, :] = jnp.zeros((KI_WIDTH - IDX_DIM, hl), BF16)

    per_tile = ATT_TK // Q_BLOCK
    n_full = (qb + 1) // per_tile
    rem = (qb + 1) % per_tile
    n_att = n_full + jnp.minimum(rem, 1)
    q_chunk = jnp.right_shift(qb * Q_BLOCK + lax.broadcasted_iota(jnp.int32, (1, lanes), 1), CHUNK_SHIFT)
    i16_min = -(2 ** 15)

    def idx_tile(j, rows):
        off = pl.multiple_of(j * ATT_TK, ATT_TK)
        kt = kidx_ref[pl.ds(off, rows), :]
        sc = jnp.zeros((rows, lanes), F32)
        for hp in range(IDX_HEADS // 2):
            z = _dot(kt, qicat[:, hp * 2 * lanes:(hp + 1) * 2 * lanes])
            for u in range(2):
                h = 2 * hp + u
                sc = sc + wt_ref[h:h + 1, :] * jnp.maximum(z[:, u * lanes:(u + 1) * lanes], 0.0)
        k_chunk = jnp.right_shift(off + lax.broadcasted_iota(jnp.int32, (rows, lanes), 0), CHUNK_SHIFT)
        bits = lax.bitcast_convert_type(sc, jnp.int32)
        key = jnp.where(bits < 0, bits ^ 0x7FFFFFFF, bits)
        key = jnp.where(k_chunk <= q_chunk, key, INT_MIN)
        keys[pl.ds(off, rows), :] = key
        half[pl.ds(off, rows), :] = jnp.right_shift(key, 16).astype(jnp.int16)
        if rows < ATT_TK:
            keys[pl.ds(off + rows, ATT_TK - rows), :] = jnp.full((ATT_TK - rows, lanes), INT_MIN, jnp.int32)
            half[pl.ds(off + rows, ATT_TK - rows), :] = jnp.full((ATT_TK - rows, lanes), i16_min, jnp.int16)

    def idx_body(j, carry):
        idx_tile(j, ATT_TK)
        return carry

    lax.fori_loop(0, n_full, idx_body, 0)
    for v in range(1, per_tile):
        pl.when(rem == v)(functools.partial(idx_tile, n_full, v * Q_BLOCK))


    def tree_sum(parts):
        while len(parts) > 1:
            parts = [a + b for a, b in zip(parts[0::2], parts[1::2])] + parts[len(parts) & ~1:]
        return parts[0]

    def threshold_search(n_units):
        n_rows = n_units * SEARCH_ROWS

        def search16(need):
            def bit_body(it, t):
                cand = t + jnp.left_shift(jnp.int32(1), 15 - it)
                cand16 = jnp.broadcast_to(cand, (16, lanes)).astype(jnp.int16)
                parts = [jnp.zeros((16, lanes), jnp.int16)] * COUNT_CHAINS
                for i, r0 in enumerate(range(0, n_rows, 16)):
                    ge = jnp.where(half[r0:r0 + 16, :] >= cand16, jnp.int16(1), jnp.int16(0))
                    parts[i % COUNT_CHAINS] = parts[i % COUNT_CHAINS] + ge
                cnt = jnp.sum(tree_sum(parts).astype(jnp.int32), axis=0, keepdims=True)
                return jnp.where(cnt >= need, cand, t)

            return lax.fori_loop(0, 16, bit_body, jnp.full((1, lanes), i16_min, jnp.int32))

        t_hi = search16(jnp.full((1, lanes), n_sel, jnp.int32))
        above = []
        for r0 in range(0, n_rows, 64):
            k = keys[r0:r0 + 64, :]
            hi = jnp.right_shift(k, 16)
            lo = (k & 0xFFFF) + i16_min
            half[r0:r0 + 64, :] = jnp.where(hi == t_hi, lo, i16_min).astype(jnp.int16)
            above.append(jnp.sum(jnp.where(hi > t_hi, 1, 0).reshape(8, 8, lanes), axis=0))
        t_lo = search16(n_sel - jnp.sum(tree_sum(above), axis=0, keepdims=True))
        thr = jnp.left_shift(t_hi, 16) + (t_lo - i16_min)
        thr_scr[...] = jnp.maximum(thr, INT_MIN + 1)

    n_search = ((qb + 1) * Q_BLOCK + SEARCH_ROWS - 1) // SEARCH_ROWS
    for n_units in range(1, keys.shape[0] // SEARCH_ROWS + 1):
        pl.when(n_search == n_units)(functools.partial(threshold_search, n_units))
    thr = thr_scr[...]

    def scores(j, rows):
        off = pl.multiple_of(j * ATT_TK, ATT_TK)
        kv = ckv_ref[pl.ds(off, rows), :]
        q_pos = qb * Q_BLOCK + lax.broadcasted_iota(jnp.int32, (rows, lanes), 1)
        k_pos = off + lax.broadcasted_iota(jnp.int32, (rows, lanes), 0)
        dist = jnp.where(keys[pl.ds(off, rows), :] >= thr, jnp.abs(q_pos - k_pos).astype(F32), MASKED_DIST)
        for hp in range(SA_HEADS // 2):
            s2 = _dot(kv, qcat[:, hp * 2 * lanes:(hp + 1) * 2 * lanes])
            for u in range(2):
                h = 2 * hp + u
                yield h * lanes, s2[:, u * lanes:(u + 1) * lanes] - slopes[h] * dist

    def reset():
        m_scr[...] = jnp.full((1, hl), M_INIT, F32)
        acc[...] = jnp.zeros((KVT_ROWS, hl), F32)

    def safe_tile(j, rows=ATT_TK):
        for lo, s in scores(j, rows):
            m_old = m_scr[:, lo:lo + lanes]
            m_new = jnp.maximum(m_old, jnp.max(s, axis=0, keepdims=True))
            a_scr[:, lo:lo + lanes] = jnp.exp2(m_old - m_new)
            m_scr[:, lo:lo + lanes] = m_new
            pbuf[0:rows, lo:lo + lanes] = jnp.exp2(s - m_new).astype(BF16)
        kvt = ckvt_ref[j, :, 0:rows]
        for hp in range(SA_HEADS // 2):
            cs = slice(hp * 2 * lanes, (hp + 1) * 2 * lanes)
            acc[:, cs] = acc[:, cs] * a_scr[:, cs] + _dot(kvt, pbuf[0:rows, cs])

    def fast_tile(j):
        for lo, s in scores(j, ATT_TK):
            ref = m_scr[:, lo:lo + lanes]
            pbuf[:, lo:lo + lanes] = jnp.exp2(s - ref).astype(BF16)
            m_tile = jnp.max(s, axis=0, keepdims=True)
            m_new = jnp.maximum(ref, m_tile)
            a_scr[:, lo:lo + lanes] = jnp.exp2(ref - m_new)
            m_scr[:, lo:lo + lanes] = m_new
            rise_scr[:, lo:lo + lanes] = jnp.maximum(rise_scr[:, lo:lo + lanes], m_tile - ref)
        kvt = ckvt_ref[j]
        for hp in range(SA_HEADS // 2):
            cs = slice(hp * 2 * lanes, (hp + 1) * 2 * lanes)
            acc[:, cs] = (acc[:, cs] + _dot(kvt, pbuf[:, cs])) * a_scr[:, cs]

    reset()
    rise_scr[...] = jnp.zeros((1, hl), F32)
    for v in range(1, per_tile):
        pl.when(rem == v)(functools.partial(safe_tile, n_full, v * Q_BLOCK))
    n_two_pass = jnp.minimum(n_full, 2 - jnp.minimum(rem, 1))

    def two_pass_body(i, carry):
        safe_tile(n_full - 1 - i)
        return carry

    def one_pass_body(i, carry):
        fast_tile(n_full - 1 - n_two_pass - i)
        return carry

    lax.fori_loop(0, n_two_pass, two_pass_body, 0)
    lax.fori_loop(0, n_full - n_two_pass, one_pass_body, 0)

    @pl.when(jnp.max(rise_scr[...]) > RESCALE_GUARD)
    def _():
        reset()

        def redo_body(j, carry):
            safe_tile(j)
            return carry

        lax.fori_loop(0, n_att, redo_body, 0)

    for h in range(SA_HEADS):
        lo = h * lanes
        o = acc[0:SA_LATENT, lo:lo + lanes] / acc[SA_LATENT:SA_LATENT + 1, lo:lo + lanes]
        ybuf[h * SA_V_DIM:(h + 1) * SA_V_DIM, :] = _dot(wuvt_ref[h], o.astype(BF16))
    y_ref[...] = ybuf[...].T.astype(BF16)


def _attn(qt, qit, wt, kidx, ckv, ckvt, w_uvt, layer, batch, seq):
    nq = SA_HEADS * SA_LATENT
    ni = IDX_HEADS * IDX_DIM
    nblk = seq // Q_BLOCK
    hl = SA_HEADS * Q_BLOCK
    n_sel = min(TOPK_MAX, seq // 4)
    slopes = tuple(float(v) * LOG2E
                   for v in np.exp2(-8.0 * (np.arange(SA_HEADS, dtype=np.float64) + 1.0) / SA_HEADS))
    assert min(slopes) * MASKED_DIST > -2.0 * M_INIT
    return pl.pallas_call(
        functools.partial(_attn_kernel, n_sel=n_sel, slopes=slopes),
        grid=(batch, nblk),
        in_specs=[
            pl.BlockSpec((None, nq, Q_BLOCK), lambda b, i: (b, 0, i)),
            pl.BlockSpec((None, ni, Q_BLOCK), lambda b, i: (b, 0, i)),
            pl.BlockSpec((None, IDX_HEADS, Q_BLOCK), lambda b, i: (b, 0, i)),
            pl.BlockSpec((None, seq, KI_WIDTH), lambda b, i: (b, 0, 0)),
            pl.BlockSpec((None, seq, SA_LATENT), lambda b, i: (b, 0, 0)),
            pl.BlockSpec((None, seq // ATT_TK, KVT_ROWS, ATT_TK), lambda b, i: (b, 0, 0, 0)),
            pl.BlockSpec((None, SA_HEADS, SA_V_DIM, SA_LATENT), lambda b, i: (layer, 0, 0, 0)),
        ],
        out_specs=pl.BlockSpec((Q_BLOCK, SA_WIDTH), lambda b, i: (b * nblk + i, 0)),
        out_shape=jax.ShapeDtypeStruct((batch * seq, SA_WIDTH), BF16),
        scratch_shapes=[
            pltpu.VMEM((SA_LATENT, hl), BF16),
            pltpu.VMEM((KI_WIDTH, hl), BF16),
            pltpu.VMEM((seq, Q_BLOCK), jnp.int32),
            pltpu.VMEM((seq, Q_BLOCK), jnp.int16),
            pltpu.VMEM((1, Q_BLOCK), jnp.int32),
            pltpu.VMEM((ATT_TK, hl), BF16),
            pltpu.VMEM((KVT_ROWS, hl), F32),
            pltpu.VMEM((1, hl), F32),
            pltpu.VMEM((1, hl), F32),
            pltpu.VMEM((1, hl), F32),
            pltpu.VMEM((SA_WIDTH, Q_BLOCK), F32),
        ],
        compiler_params=_params("arbitrary", "arbitrary"),
        name="dsa_attn",
    )(qt, qit, wt, kidx, ckv, ckvt, w_uvt)


def _merge_kernel(ya_ref, yb_ref, ga_ref, gb_ref, x_ref, gt_ref, wpa_ref, wpb_ref, wout_ref,
                  lng_ref, lnb_ref, o_ref, mbuf, zbuf, *, alpha, tc):
    d = x_ref.shape[1]
    ya = ya_ref[...]
    yb = yb_ref[...]
    for c0 in range(0, d, tc):
        cs = slice(c0, c0 + tc)
        m = (_sigmoid(ga_ref[:, cs]) * _dot(ya, wpa_ref[:, cs])
             + _sigmoid(gb_ref[:, cs]) * _dot(yb, wpb_ref[:, cs]))
        mbuf[:, cs] = m.astype(BF16)
    mm = mbuf[...]
    for c0 in range(0, d, tc):
        cs = slice(c0, c0 + tc)
        zbuf[:, cs] = alpha * x_ref[:, cs] + (1.0 + gt_ref[:, cs]) * _dot(mm, wout_ref[:, cs])
    o_ref[...] = _layer_norm(zbuf[...], lng_ref[...], lnb_ref[...])


def _merge(ya, yb, proj, x2d, gt, wpa, wpb, wout, lng, lnb, layer, seq, col_ga, alpha, tm):
    t, d = x2d.shape
    per_b = seq // tm
    ga_blk = col_ga // d
    const = lambda i: (0, 0)
    wsel = lambda i: (layer, 0, 0)
    return pl.pallas_call(
        functools.partial(_merge_kernel, alpha=alpha, tc=min(d, 512)),
        grid=(t // tm,),
        in_specs=[
            pl.BlockSpec((tm, HG_WIDTH), lambda i: (i, 0)),
            pl.BlockSpec((tm, SA_WIDTH), lambda i: (i, 0)),
            pl.BlockSpec((tm, d), lambda i: (i, ga_blk)),
            pl.BlockSpec((tm, d), lambda i: (i, ga_blk + 1)),
            pl.BlockSpec((tm, d), lambda i: (i, 0)),
            pl.BlockSpec((None, 1, d), lambda i: (i // per_b, 0, 0)),
            pl.BlockSpec((None, HG_WIDTH, d), wsel, pipeline_mode=pl.Buffered(1)),
            pl.BlockSpec((None, SA_WIDTH, d), wsel, pipeline_mode=pl.Buffered(1)),
            pl.BlockSpec((None, d, d), wsel, pipeline_mode=pl.Buffered(1)),
            pl.BlockSpec((1, d), const),
            pl.BlockSpec((1, d), const),
        ],
        out_specs=pl.BlockSpec((tm, d), lambda i: (i, 0)),
        out_shape=jax.ShapeDtypeStruct((t, d), F32),
        scratch_shapes=[pltpu.VMEM((tm, d), BF16), pltpu.VMEM((tm, d), F32)],
        compiler_params=_params("arbitrary"),
        name="merge_out",
    )(ya, yb, proj, proj, x2d, gt, wpa, wpb, wout, lng.reshape(1, d), lnb.reshape(1, d))


def _swiglu_kernel(x_ref, sc_ref, sh_ref, gt_ref, wg_ref, wu_ref, wd_ref, lng_ref, lnb_ref,
                   o_ref, hbuf, acc, *, alpha):
    f = pl.program_id(1)

    @pl.when(f == 0)
    def _():
        hbuf[...] = (x_ref[...] * (1.0 + sc_ref[...]) + sh_ref[...]).astype(BF16)
        acc[...] = jnp.zeros_like(acc)

    h = hbuf[...]
    g = _dot(h, wg_ref[...])
    u = _dot(h, wu_ref[...])
    a = (g * _sigmoid(g) * u).astype(BF16)
    acc[...] += _dot(a, wd_ref[...])

    @pl.when(f == pl.num_programs(1) - 1)
    def _():
        z = alpha * x_ref[...] + (1.0 + gt_ref[...]) * acc[...]
        o_ref[...] = _layer_norm(z, lng_ref[...], lnb_ref[...])


def _swiglu(x2d, sc, sh, gt, wg, wu, wd, lng, lnb, layer, seq, alpha, tm, tf):
    t, d = x2d.shape
    dff = wg.shape[2]
    per_b = seq // tm
    mod = pl.BlockSpec((None, 1, d), lambda i, f: (i // per_b, 0, 0))
    const = pl.BlockSpec((1, d), lambda i, f: (0, 0))
    return pl.pallas_call(
        functools.partial(_swiglu_kernel, alpha=alpha),
        grid=(t // tm, dff // tf),
        in_specs=[
            pl.BlockSpec((tm, d), lambda i, f: (i, 0)),
            mod, mod, mod,
            pl.BlockSpec((None, d, tf), lambda i, f: (layer, 0, f)),
            pl.BlockSpec((None, d, tf), lambda i, f: (layer, 0, f)),
            pl.BlockSpec((None, tf, d), lambda i, f: (layer, f, 0)),
            const, const,
        ],
        out_specs=pl.BlockSpec((tm, d), lambda i, f: (i, 0)),
        out_shape=jax.ShapeDtypeStruct((t, d), F32),
        scratch_shapes=[pltpu.VMEM((tm, d), BF16), pltpu.VMEM((tm, d), F32)],
        compiler_params=_params("arbitrary", "arbitrary"),
        name="swiglu",
    )(x2d, sc, sh, gt, wg, wu, wd, lng.reshape(1, d), lnb.reshape(1, d))


def _regroup_kernel(w_ref, o_ref, *, groups):
    row = 0
    for lo, hi in groups:
        o_ref[row:row + hi - lo, :] = w_ref[lo:hi, :].astype(BF16)
        row += hi - lo
    o_ref[row:, :] = jnp.zeros((o_ref.shape[0] - row, o_ref.shape[1]), BF16)


def _regroup_w_in(w_in_t, n_proj):
    depth, d_in, d = w_in_t.shape
    small = SA_Q_RANK + SA_LATENT + IDX_DIM + IDX_HEADS
    hg4 = 4 * HG_WIDTH
    groups = ((0, hg4), (hg4 + small, d_in), (hg4, hg4 + small))
    cols = 256
    return pl.pallas_call(
        functools.partial(_regroup_kernel, groups=groups),
        grid=(depth, d // cols),
        in_specs=[pl.BlockSpec((None, d_in, cols), lambda l, i: (l, 0, i))],
        out_specs=pl.BlockSpec((None, n_proj, cols), lambda l, i: (l, 0, i)),
        out_shape=jax.ShapeDtypeStruct((depth, n_proj, d), BF16),
        compiler_params=_params("arbitrary", "arbitrary"),
        name="w_in_regroup",
    )(w_in_t)


def _largest_tile(n, cap, quantum):
    best = quantum
    for cand in range(quantum, min(n, cap) + 1, quantum):
        if n % cand == 0:
            best = cand
    return best


def kernel(x, c, w_ada, b_ada, w_in, lb_logits, g_hg, w_pa, g_cq, g_ckv, w_uq, w_iq, w_uv, w_pb,
           w_out, ln1_g, ln1_b, w_gate, w_up, w_down, ln2_g, ln2_b):
    batch, seq, d = x.shape
    depth = w_ada.shape[0]
    dff = w_gate.shape[2]
    alpha = (2 * depth) ** 0.25
    assert seq % ATT_TK == 0 and d % 256 == 0 and batch <= ADA_PAD_ROWS

    hg4 = 4 * HG_WIDTH
    col_ga = hg4
    col_cq = hg4 + 2 * d
    used = col_cq + SA_Q_RANK + SA_LATENT + KI_WIDTH
    tn = 1536 if d >= 1024 else 512
    n_proj = -(-used // tn) * tn
    w_in_r = _regroup_w_in(jnp.swapaxes(w_in, 1, 2), n_proj)

    scale = SA_LATENT ** -0.5 * LOG2E
    w_uq_t = _transpose_weights(w_uq, scale)
    w_iq_t = _transpose_weights(w_iq, 1.0)
    w_uvt = jnp.swapaxes(w_uv, 2, 3).astype(BF16)
    w_pa_b, w_pb_b, w_out_b = w_pa.astype(BF16), w_pb.astype(BF16), w_out.astype(BF16)
    w_gate_b, w_up_b, w_down_b = w_gate.astype(BF16), w_up.astype(BF16), w_down.astype(BF16)

    p = jax.nn.softmax(lb_logits.astype(F32), axis=0)
    lbs = jnp.cumsum(p, axis=0) - p[0:1]

    c_pad = jnp.zeros((ADA_PAD_ROWS, d), F32).at[:batch].set(c)
    mods = _ada(c_pad, w_ada, b_ada)[:, :batch].reshape(depth, batch, 6, 1, d)

    tm_proj = _largest_tile(seq, 1024, 256)
    lblk = _largest_tile(seq, 1024, HGRN_SUB)
    tm_prep = ATT_TK
    tm_merge = 256
    tm_ffn = _largest_tile(seq, 512, 256)
    tf = _largest_tile(dff, 512, 128)

    x2d = x.reshape(batch * seq, d)
    for l in range(depth):
        sh_m, sc_m, gt_m, sh_f, sc_f, gt_f = (mods[l, :, k] for k in range(6))
        proj = _inproj(x2d, sc_m, sh_m, w_in_r, l, seq, tm_proj, tn)
        y_a = _hgrn(proj, lbs[l], g_hg[l], batch, seq, lblk)
        qt, qit, kidx, ckv, ckvt, wt = _dsa_prep(proj, g_cq[l], g_ckv[l], w_uq_t, w_iq_t, l, batch, seq, col_cq, tm_prep)
        y_b = _attn(qt, qit, wt, kidx, ckv, ckvt, w_uvt, l, batch, seq)
        x2d = _merge(y_a, y_b, proj, x2d, gt_m, w_pa_b, w_pb_b, w_out_b, ln1_g[l], ln1_b[l],
                     l, seq, col_ga, alpha, tm_merge)
        x2d = _swiglu(x2d, sc_f, sh_f, gt_f, w_gate_b, w_up_b, w_down_b, ln2_g[l], ln2_b[l],
                      l, seq, alpha, tm_ffn, tf)
    return x2d.reshape(batch, seq, d)
```

```python
import functools

import numpy as np
import jax
import jax.numpy as jnp
from jax import lax
from jax.experimental import pallas as pl
from jax.experimental.pallas import tpu as pltpu

F32 = jnp.float32
BF16 = jnp.bfloat16

HG_HEADS = 8
HG_DIM = 128
F_MIN = 1e-6
SA_HEADS = 16
SA_LATENT = 256
SA_Q_RANK = 512
SA_V_DIM = 64
IDX_HEADS = 16
IDX_DIM = 64
TOPK_MAX = 256
CHUNK = 64
CHUNK_SHIFT = 6
Q_BLOCK = 256
ADA_PAD_ROWS = 8

HG_WIDTH = HG_HEADS * HG_DIM
SA_WIDTH = SA_HEADS * SA_V_DIM
KI_WIDTH = 128

VMEM_LIMIT = 56 * 1024 * 1024
INT_MIN = -(2 ** 31)
MASKED_DIST = 1e32
M_INIT = -1e29

HGRN_SUB = 256
HG_GROUP = 8
ATT_TK = 512
SEARCH_ROWS = 256
COUNT_CHAINS = 8
RESCALE_GUARD = 100.0
KVT_ROWS = SA_LATENT + 16
LOG2E = 1.4426950408889634


def _dot(a, b):
    return jnp.dot(a, b, preferred_element_type=F32)


def _dot_nt(a, b):
    return lax.dot_general(a, b, (((1,), (1,)), ((), ())), preferred_element_type=F32)


def _dot_tn(a, b):
    return lax.dot_general(a, b, (((0,), (0,)), ((), ())), preferred_element_type=F32)


def _sigmoid(x):
    return 1.0 / (1.0 + jnp.exp(-x))


def _layer_norm(z, g, b):
    mu = jnp.mean(z, axis=-1, keepdims=True)
    zc = z - mu
    var = jnp.mean(zc * zc, axis=-1, keepdims=True)
    return zc * lax.rsqrt(var + 1e-5) * g + b


def _params(*sem):
    return pltpu.CompilerParams(dimension_semantics=sem, vmem_limit_bytes=VMEM_LIMIT)


def _ada_kernel(c_ref, w_ref, b_ref, o_ref):
    c = c_ref[...]
    cond = c * _sigmoid(c)
    w = w_ref[...]
    c_hi = cond.astype(BF16)
    c_lo = (cond - c_hi.astype(F32)).astype(BF16)
    w_hi = w.astype(BF16)
    w_lo = (w - w_hi.astype(F32)).astype(BF16)
    acc = _dot(c_hi, w_hi) + _dot(c_hi, w_lo) + _dot(c_lo, w_hi)
    o_ref[...] = acc + b_ref[...]


def _ada(c_pad, w_ada, b_ada):
    depth, d, n = w_ada.shape
    tn = _largest_tile(n, 1024, 128)
    return pl.pallas_call(
        _ada_kernel,
        grid=(depth, n // tn),
        in_specs=[
            pl.BlockSpec((ADA_PAD_ROWS, d), lambda l, j: (0, 0)),
            pl.BlockSpec((None, d, tn), lambda l, j: (l, 0, j)),
            pl.BlockSpec((None, 1, tn), lambda l, j: (l, 0, j)),
        ],
        out_specs=pl.BlockSpec((None, ADA_PAD_ROWS, tn), lambda l, j: (l, 0, j)),
        out_shape=jax.ShapeDtypeStruct((depth, ADA_PAD_ROWS, n), F32),
        compiler_params=_params("arbitrary", "arbitrary"),
        name="ada_mod",
    )(c_pad, w_ada, b_ada.reshape(depth, 1, n))


def _inproj_kernel(x_ref, sc_ref, sh_ref, w_ref, o_ref, h_ref):
    @pl.when(pl.program_id(1) == 0)
    def _():
        h_ref[...] = (x_ref[...] * (1.0 + sc_ref[...]) + sh_ref[...]).astype(BF16)

    o_ref[...] = _dot_nt(h_ref[...], w_ref[...])


def _inproj(x2d, sc, sh, w, layer, seq, tm, tn):
    t, d = x2d.shape
    n = w.shape[1]
    per_b = seq // tm
    return pl.pallas_call(
        _inproj_kernel,
        grid=(t // tm, n // tn),
        in_specs=[
            pl.BlockSpec((tm, d), lambda i, j: (i, 0)),
            pl.BlockSpec((None, 1, d), lambda i, j: (i // per_b, 0, 0)),
            pl.BlockSpec((None, 1, d), lambda i, j: (i // per_b, 0, 0)),
            pl.BlockSpec((None, tn, d), lambda i, j: (layer, j, 0)),
        ],
        out_specs=pl.BlockSpec((tm, tn), lambda i, j: (i, j)),
        out_shape=jax.ShapeDtypeStruct((t, n), F32),
        scratch_shapes=[pltpu.VMEM((tm, d), BF16)],
        compiler_params=_params("arbitrary", "arbitrary"),
        name="in_proj",
    )(x2d, sc, sh, w)


def _hgrn_kernel(q_ref, f_ref, v_ref, g_ref, lb_ref, gh_ref, y_ref, st_ref, *, nsub):
    sub = HGRN_SUB

    @pl.when(pl.program_id(2) == 0)
    def _():
        st_ref[...] = jnp.zeros_like(st_ref)

    row = lax.broadcasted_iota(jnp.int32, (sub, HG_DIM), 0)
    ti = lax.broadcasted_iota(jnp.int32, (sub, sub), 0)
    si = lax.broadcasted_iota(jnp.int32, (sub, sub), 1)
    tx = ti ^ si
    tril = jnp.where(si <= ti, 1.0, 0.0).astype(BF16)
    diag_keep = jnp.where(si <= ti, tx, 8) < 8

    def head_block(off, hh):
        cols = slice(hh * HG_DIM, (hh + 1) * HG_DIM)
        lb = lb_ref[:, cols]
        gh = gh_ref[:, cols]
        q = q_ref[pl.ds(off, sub), cols]
        fa = f_ref[pl.ds(off, sub), cols]
        v = v_ref[pl.ds(off, sub), cols].astype(BF16)
        ga = g_ref[pl.ds(off, sub), cols]

        f = lb + (1.0 - lb) * _sigmoid(fa)
        logf = jnp.log(jnp.clip(f, F_MIN, 1.0))
        kk = 1.0 - f

        l_hi = logf.astype(BF16)
        r1 = logf - l_hi.astype(F32)
        l_mid = r1.astype(BF16)
        l_lo = (r1 - l_mid.astype(F32)).astype(BF16)
        b = _dot(tril, l_hi) + _dot(tril, l_mid) + _dot(tril, l_lo)

        a = jnp.zeros((sub, sub), F32)
        h = sub // 2
        while h >= 8:
            n = sub // (2 * h)
            b3 = b.reshape(n, 2 * h, HG_DIM)
            e = jnp.exp(-jnp.abs(b3 - b3[:, h - 1:h, :])).reshape(sub, HG_DIM)
            second = (row & h) != 0
            qd = jnp.where(second, q * e, 0.0).astype(BF16)
            kd = jnp.where(second, 0.0, kk * e).astype(BF16)
            p = _dot_nt(qd, kd)
            a = a + (p if 2 * h == sub else jnp.where(tx < 2 * h, p, 0.0))
            h //= 2
        b3 = b.reshape(sub // 8, 8, HG_DIM)
        xd = (b3 - b3[:, 3:4, :]).reshape(sub, HG_DIM)
        qd = (q * jnp.exp(xd)).astype(BF16)
        kd = (kk * jnp.exp(-xd)).astype(BF16)
        a = a + jnp.where(diag_keep, _dot_nt(qd, kd), 0.0)

        st = st_ref[hh]
        qi = (q * jnp.exp(b)).astype(BF16)
        o = _dot_nt(qi, st.astype(BF16)) + _dot(a.astype(BF16), v)
        b_last = b[sub - 1:sub, :]
        kdi = (kk * jnp.exp(b_last - b)).astype(BF16)
        st_ref[hh] = st * jnp.exp(b_last) + _dot_tn(v, kdi)

        ms = jnp.mean(o * o, axis=-1, keepdims=True)
        y = o * lax.rsqrt(ms + 1e-6) * gh * (ga * _sigmoid(ga))
        y_ref[pl.ds(off, sub), cols] = y.astype(BF16)

    def body(r, carry):
        off = pl.multiple_of(r * sub, sub)
        for hh in range(HG_GROUP):
            head_block(off, hh)
        return carry

    lax.fori_loop(0, nsub, body, 0)


def _hgrn(proj, lb, g_hg, batch, seq, lblk):
    t = proj.shape[0]
    nl = seq // lblk
    ng = HG_HEADS // HG_GROUP
    gw = HG_GROUP * HG_DIM
    col = lambda c0: (lambda b, h, l: (b * nl + l, c0 + h))
    blk = lambda c0: pl.BlockSpec((lblk, gw), col(c0))
    vec = pl.BlockSpec((None, 1, gw), lambda b, h, l: (h, 0, 0))
    return pl.pallas_call(
        functools.partial(_hgrn_kernel, nsub=lblk // HGRN_SUB),
        grid=(batch, ng, nl),
        in_specs=[blk(0), blk(ng), blk(2 * ng), blk(3 * ng), vec, vec],
        out_specs=pl.BlockSpec((lblk, gw), col(0)),
        out_shape=jax.ShapeDtypeStruct((t, HG_WIDTH), BF16),
        scratch_shapes=[pltpu.VMEM((HG_GROUP, HG_DIM, HG_DIM), F32)],
        compiler_params=_params("arbitrary", "arbitrary", "arbitrary"),
        name="hgrn2",
    )(proj, proj, proj, proj, lb.reshape(ng, 1, gw), g_hg.reshape(ng, 1, gw))


def _wt_kernel(w_ref, o_ref, *, scale):
    o_ref[...] = (w_ref[...].T * scale).astype(BF16)


def _transpose_weights(w, scale):
    depth, r, n = w.shape
    tn = _largest_tile(n, 512, 128)
    return pl.pallas_call(
        functools.partial(_wt_kernel, scale=scale),
        grid=(depth, n // tn),
        in_specs=[pl.BlockSpec((None, r, tn), lambda l, j: (l, 0, j))],
        out_specs=pl.BlockSpec((None, tn, r), lambda l, j: (l, j, 0)),
        out_shape=jax.ShapeDtypeStruct((depth, n, r), BF16),
        compiler_params=_params("arbitrary", "arbitrary"),
        name="w_transpose",
    )(w)


def _dsa_prep_kernel(cq_ref, ckv_ref, ki_ref, gcq_ref, gckv_ref, wqt_ref, wit_ref,
                     qt_ref, qit_ref, kidx_ref, ckvo_ref, ckvt_ref, wt_o_ref):
    cq = cq_ref[...]
    cqn = cq * lax.rsqrt(jnp.mean(cq * cq, axis=-1, keepdims=True) + 1e-6) * gcq_ref[...]
    cqn_t = cqn.T.astype(BF16)
    step = 512
    for c0 in range(0, SA_HEADS * SA_LATENT, step):
        qt_ref[c0:c0 + step, :] = _dot(wqt_ref[c0:c0 + step, :], cqn_t).astype(BF16)
    for c0 in range(0, IDX_HEADS * IDX_DIM, step):
        qit_ref[c0:c0 + step, :] = _dot(wit_ref[c0:c0 + step, :], cqn_t).astype(BF16)

    ckv = ckv_ref[...]
    ckvn = ckv * lax.rsqrt(jnp.mean(ckv * ckv, axis=-1, keepdims=True) + 1e-6) * gckv_ref[...]
    ckvo_ref[...] = ckvn.astype(BF16)
    ckvt_ref[0:SA_LATENT, :] = ckvn.T.astype(BF16)
    ckvt_ref[SA_LATENT:KVT_ROWS, :] = jnp.ones((KVT_ROWS - SA_LATENT, ckvt_ref.shape[1]), BF16)

    ki = ki_ref[...]
    kidx_ref[...] = ki.astype(BF16)
    ki_t = ki.T
    wt_o_ref[...] = ki_t[IDX_DIM:IDX_DIM + IDX_HEADS, :] * (IDX_HEADS ** -0.5 * IDX_DIM ** -0.5)


def _dsa_prep(proj, g_cq, g_ckv, w_uq_t, w_iq_t, layer, batch, seq, col_cq, tm):
    nq = SA_HEADS * SA_LATENT
    ni = IDX_HEADS * IDX_DIM
    per_b = seq // tm
    cq_blk = col_cq // SA_Q_RANK
    ckv_blk = (col_cq + SA_Q_RANK) // SA_LATENT
    ki_blk = (col_cq + SA_Q_RANK + SA_LATENT) // KI_WIDTH
    tok = lambda i: (i // per_b, i % per_b)
    return pl.pallas_call(
        _dsa_prep_kernel,
        grid=(batch * per_b,),
        in_specs=[
            pl.BlockSpec((tm, SA_Q_RANK), lambda i: (i, cq_blk)),
            pl.BlockSpec((tm, SA_LATENT), lambda i: (i, ckv_blk)),
            pl.BlockSpec((tm, KI_WIDTH), lambda i: (i, ki_blk)),
            pl.BlockSpec((1, SA_Q_RANK), lambda i: (0, 0)),
            pl.BlockSpec((1, SA_LATENT), lambda i: (0, 0)),
            pl.BlockSpec((None, nq, SA_Q_RANK), lambda i: (layer, 0, 0)),
            pl.BlockSpec((None, ni, SA_Q_RANK), lambda i: (layer, 0, 0)),
        ],
        out_specs=[
            pl.BlockSpec((None, nq, tm), lambda i: (tok(i)[0], 0, tok(i)[1])),
            pl.BlockSpec((None, ni, tm), lambda i: (tok(i)[0], 0, tok(i)[1])),
            pl.BlockSpec((None, tm, KI_WIDTH), lambda i: (tok(i)[0], tok(i)[1], 0)),
            pl.BlockSpec((None, tm, SA_LATENT), lambda i: (tok(i)[0], tok(i)[1], 0)),
            pl.BlockSpec((None, None, KVT_ROWS, tm), lambda i: (tok(i)[0], tok(i)[1], 0, 0)),
            pl.BlockSpec((None, IDX_HEADS, tm), lambda i: (tok(i)[0], 0, tok(i)[1])),
        ],
        out_shape=[
            jax.ShapeDtypeStruct((batch, nq, seq), BF16),
            jax.ShapeDtypeStruct((batch, ni, seq), BF16),
            jax.ShapeDtypeStruct((batch, seq, KI_WIDTH), BF16),
            jax.ShapeDtypeStruct((batch, seq, SA_LATENT), BF16),
            jax.ShapeDtypeStruct((batch, per_b, KVT_ROWS, tm), BF16),
            jax.ShapeDtypeStruct((batch, IDX_HEADS, seq), F32),
        ],
        compiler_params=_params("arbitrary"),
        name="dsa_prep",
    )(proj, proj, proj, g_cq.reshape(1, -1), g_ckv.reshape(1, -1), w_uq_t, w_iq_t)


def _attn_kernel(qt_ref, qit_ref, wt_ref, kidx_ref, ckv_ref, ckvt_ref, wuvt_ref, y_ref,
                 qcat, qicat, keys, half, thr_scr, pbuf, acc, m_scr, a_scr, rise_scr, ybuf, *, n_sel, slopes):
    qb = pl.program_id(1)
    lanes = Q_BLOCK
    hl = SA_HEADS * lanes

    for h in range(SA_HEADS):
        qcat[:, h * lanes:(h + 1) * lanes] = qt_ref[h * SA_LATENT:(h + 1) * SA_LATENT, :]
        qicat[0:IDX_DIM, h * lanes:(h + 1) * lanes] = qit_ref[h * IDX_DIM:(h + 1) * IDX_DIM, :]
    qicat[IDX_DIM:KI_WIDTH, :] = jnp.zeros((KI_WIDTH - IDX_DIM, hl), BF16)

    per_tile = ATT_TK // Q_BLOCK
    n_full = (qb + 1) // per_tile
    rem = (qb + 1) % per_tile
    n_att = n_full + jnp.minimum(rem, 1)
    q_chunk = jnp.right_shift(qb * Q_BLOCK + lax.broadcasted_iota(jnp.int32, (1, lanes), 1), CHUNK_SHIFT)
    i16_min = -(2 ** 15)

    def idx_tile(j, rows):
        off = pl.multiple_of(j * ATT_TK, ATT_TK)
        kt = kidx_ref[pl.ds(off, rows), :]
        sc = jnp.zeros((rows, lanes), F32)
        for hp in range(IDX_HEADS // 2):
            z = _dot(kt, qicat[:, hp * 2 * lanes:(hp + 1) * 2 * lanes])
            for u in range(2):
                h = 2 * hp + u
                sc = sc + wt_ref[h:h + 1, :] * jnp.maximum(z[:, u * lanes:(u + 1) * lanes], 0.0)
        k_chunk = jnp.right_shift(off + lax.broadcasted_iota(jnp.int32, (rows, lanes), 0), CHUNK_SHIFT)
        bits = lax.bitcast_convert_type(sc, jnp.int32)
        key = jnp.where(bits < 0, bits ^ 0x7FFFFFFF, bits)
        key = jnp.where(k_chunk <= q_chunk, key, INT_MIN)
        keys[pl.ds(off, rows), :] = key
        half[pl.ds(off, rows), :] = jnp.right_shift(key, 16).astype(jnp.int16)
        if rows < ATT_TK:
            keys[pl.ds(off + rows, ATT_TK - rows), :] = jnp.full((ATT_TK - rows, lanes), INT_MIN, jnp.int32)
            half[pl.ds(off + rows, ATT_TK - rows), :] = jnp.full((ATT_TK - rows, lanes), i16_min, jnp.int16)

    def idx_body(j, carry):
        idx_tile(j, ATT_TK)
        return carry

    lax.fori_loop(0, n_full, idx_body, 0)
    for v in range(1, per_tile):
        pl.when(rem == v)(functools.partial(idx_tile, n_full, v * Q_BLOCK))


    def tree_sum(parts):
        while len(parts) > 1:
            parts = [a + b for a, b in zip(parts[0::2], parts[1::2])] + parts[len(parts) & ~1:]
        return parts[0]

    def threshold_search(n_units):
        n_rows = n_units * SEARCH_ROWS

        def search16(need):
            def bit_body(it, t):
                cand = t + jnp.left_shift(jnp.int32(1), 15 - it)
                cand16 = jnp.broadcast_to(cand, (16, lanes)).astype(jnp.int16)
                parts = [jnp.zeros((16, lanes), jnp.int16)] * COUNT_CHAINS
                for i, r0 in enumerate(range(0, n_rows, 16)):
                    ge = jnp.where(half[r0:r0 + 16, :] >= cand16, jnp.int16(1), jnp.int16(0))
                    parts[i % COUNT_CHAINS] = parts[i % COUNT_CHAINS] + ge
                cnt = jnp.sum(tree_sum(parts).astype(jnp.int32), axis=0, keepdims=True)
                return jnp.where(cnt >= need, cand, t)

            return lax.fori_loop(0, 16, bit_body, jnp.full((1, lanes), i16_min, jnp.int32))

        t_hi = search16(jnp.full((1, lanes), n_sel, jnp.int32))
        above = []
        for r0 in range(0, n_rows, 64):
            k = keys[r0:r0 + 64, :]
            hi = jnp.right_shift(k, 16)
            lo = (k & 0xFFFF) + i16_min
            half[r0:r0 + 64, :] = jnp.where(hi == t_hi, lo, i16_min).astype(jnp.int16)
            above.append(jnp.sum(jnp.where(hi > t_hi, 1, 0).reshape(8, 8, lanes), axis=0))
        t_lo = search16(n_sel - jnp.sum(tree_sum(above), axis=0, keepdims=True))
        thr = jnp.left_shift(t_hi, 16) + (t_lo - i16_min)
        thr_scr[...] = jnp.maximum(thr, INT_MIN + 1)

    n_search = ((qb + 1) * Q_BLOCK + SEARCH_ROWS - 1) // SEARCH_ROWS
    for n_units in range(1, keys.shape[0] // SEARCH_ROWS + 1):
        pl.when(n_search == n_units)(functools.partial(threshold_search, n_units))
    thr = thr_scr[...]

    def scores(j, rows):
        off = pl.multiple_of(j * ATT_TK, ATT_TK)
        kv = ckv_ref[pl.ds(off, rows), :]
        q_pos = qb * Q_BLOCK + lax.broadcasted_iota(jnp.int32, (rows, lanes), 1)
        k_pos = off + lax.broadcasted_iota(jnp.int32, (rows, lanes), 0)
        dist = jnp.where(keys[pl.ds(off, rows), :] >= thr, jnp.abs(q_pos - k_pos).astype(F32), MASKED_DIST)
        for hp in range(SA_HEADS // 2):
            s2 = _dot(kv, qcat[:, hp * 2 * lanes:(hp + 1) * 2 * lanes])
            for u in range(2):
                h = 2 * hp + u
                yield h * lanes, s2[:, u * lanes:(u + 1) * lanes] - slopes[h] * dist

    def reset():
        m_scr[...] = jnp.full((1, hl), M_INIT, F32)
        acc[...] = jnp.zeros((KVT_ROWS, hl), F32)

    def safe_tile(j, rows=ATT_TK):
        for lo, s in scores(j, rows):
            m_old = m_scr[:, lo:lo + lanes]
            m_new = jnp.maximum(m_old, jnp.max(s, axis=0, keepdims=True))
            a_scr[:, lo:lo + lanes] = jnp.exp2(m_old - m_new)
            m_scr[:, lo:lo + lanes] = m_new
            pbuf[0:rows, lo:lo + lanes] = jnp.exp2(s - m_new).astype(BF16)
        kvt = ckvt_ref[j, :, 0:rows]
        for hp in range(SA_HEADS // 2):
            cs = slice(hp * 2 * lanes, (hp + 1) * 2 * lanes)
            acc[:, cs] = acc[:, cs] * a_scr[:, cs] + _dot(kvt, pbuf[0:rows, cs])

    def fast_tile(j):
        for lo, s in scores(j, ATT_TK):
            ref = m_scr[:, lo:lo + lanes]
            pbuf[:, lo:lo + lanes] = jnp.exp2(s - ref).astype(BF16)
            m_tile = jnp.max(s, axis=0, keepdims=True)
            m_new = jnp.maximum(ref, m_tile)
            a_scr[:, lo:lo + lanes] = jnp.exp2(ref - m_new)
            m_scr[:, lo:lo + lanes] = m_new
            rise_scr[:, lo:lo + lanes] = jnp.maximum(rise_scr[:, lo:lo + lanes], m_tile - ref)
        kvt = ckvt_ref[j]
        for hp in range(SA_HEADS // 2):
            cs = slice(hp * 2 * lanes, (hp + 1) * 2 * lanes)
            acc[:, cs] = (acc[:, cs] + _dot(kvt, pbuf[:, cs])) * a_scr[:, cs]

    reset()
    rise_scr[...] = jnp.zeros((1, hl), F32)
    for v in range(1, per_tile):
        pl.when(rem == v)(functools.partial(safe_tile, n_full, v * Q_BLOCK))
    n_two_pass = jnp.minimum(n_full, 2 - jnp.minimum(rem, 1))

    def two_pass_body(i, carry):
        safe_tile(n_full - 1 - i)
        return carry

    def one_pass_body(i, carry):
        fast_tile(n_full - 1 - n_two_pass - i)
        return carry

    lax.fori_loop(0, n_two_pass, two_pass_body, 0)
    lax.fori_loop(0, n_full - n_two_pass, one_pass_body, 0)

    @pl.when(jnp.max(rise_scr[...]) > RESCALE_GUARD)
    def _():
        reset()

        def redo_body(j, carry):
            safe_tile(j)
            return carry

        lax.fori_loop(0, n_att, redo_body, 0)

    for h in range(SA_HEADS):
        lo = h * lanes
        o = acc[0:SA_LATENT, lo:lo + lanes] / acc[SA_LATENT:SA_LATENT + 1, lo:lo + lanes]
        ybuf[h * SA_V_DIM:(h + 1) * SA_V_DIM, :] = _dot(wuvt_ref[h], o.astype(BF16))
    y_ref[...] = ybuf[...].T.astype(BF16)


def _attn(qt, qit, wt, kidx, ckv, ckvt, w_uvt, layer, batch, seq):
    nq = SA_HEADS * SA_LATENT
    ni = IDX_HEADS * IDX_DIM
    nblk = seq // Q_BLOCK
    hl = SA_HEADS * Q_BLOCK
    n_sel = min(TOPK_MAX, seq // 4)
    slopes = tuple(float(v) * LOG2E
                   for v in np.exp2(-8.0 * (np.arange(SA_HEADS, dtype=np.float64) + 1.0) / SA_HEADS))
    assert min(slopes) * MASKED_DIST > -2.0 * M_INIT
    return pl.pallas_call(
        functools.partial(_attn_kernel, n_sel=n_sel, slopes=slopes),
        grid=(batch, nblk),
        in_specs=[
            pl.BlockSpec((None, nq, Q_BLOCK), lambda b, i: (b, 0, i)),
            pl.BlockSpec((None, ni, Q_BLOCK), lambda b, i: (b, 0, i)),
            pl.BlockSpec((None, IDX_HEADS, Q_BLOCK), lambda b, i: (b, 0, i)),
            pl.BlockSpec((None, seq, KI_WIDTH), lambda b, i: (b, 0, 0)),
            pl.BlockSpec((None, seq, SA_LATENT), lambda b, i: (b, 0, 0)),
            pl.BlockSpec((None, seq // ATT_TK, KVT_ROWS, ATT_TK), lambda b, i: (b, 0, 0, 0)),
            pl.BlockSpec((None, SA_HEADS, SA_V_DIM, SA_LATENT), lambda b, i: (layer, 0, 0, 0)),
        ],
        out_specs=pl.BlockSpec((Q_BLOCK, SA_WIDTH), lambda b, i: (b * nblk + i, 0)),
        out_shape=jax.ShapeDtypeStruct((batch * seq, SA_WIDTH), BF16),
        scratch_shapes=[
            pltpu.VMEM((SA_LATENT, hl), BF16),
            pltpu.VMEM((KI_WIDTH, hl), BF16),
            pltpu.VMEM((seq, Q_BLOCK), jnp.int32),
            pltpu.VMEM((seq, Q_BLOCK), jnp.int16),
            pltpu.VMEM((1, Q_BLOCK), jnp.int32),
            pltpu.VMEM((ATT_TK, hl), BF16),
            pltpu.VMEM((KVT_ROWS, hl), F32),
            pltpu.VMEM((1, hl), F32),
            pltpu.VMEM((1, hl), F32),
            pltpu.VMEM((1, hl), F32),
            pltpu.VMEM((SA_WIDTH, Q_BLOCK), F32),
        ],
        compiler_params=_params("arbitrary", "arbitrary"),
        name="dsa_attn",
    )(qt, qit, wt, kidx, ckv, ckvt, w_uvt)


def _merge_kernel(ya_ref, yb_ref, ga_ref, gb_ref, x_ref, gt_ref, wpa_ref, wpb_ref, wout_ref,
                  lng_ref, lnb_ref, o_ref, mbuf, zbuf, *, alpha, tc):
    d = x_ref.shape[1]
    ya = ya_ref[...]
    yb = yb_ref[...]
    for c0 in range(0, d, tc):
        cs = slice(c0, c0 + tc)
        m = (_sigmoid(ga_ref[:, cs]) * _dot(ya, wpa_ref[:, cs])
             + _sigmoid(gb_ref[:, cs]) * _dot(yb, wpb_ref[:, cs]))
        mbuf[:, cs] = m.astype(BF16)
    mm = mbuf[...]
    for c0 in range(0, d, tc):
        cs = slice(c0, c0 + tc)
        zbuf[:, cs] = alpha * x_ref[:, cs] + (1.0 + gt_ref[:, cs]) * _dot(mm, wout_ref[:, cs])
    o_ref[...] = _layer_norm(zbuf[...], lng_ref[...], lnb_ref[...])


def _merge(ya, yb, proj, x2d, gt, wpa, wpb, wout, lng, lnb, layer, seq, col_ga, alpha, tm):
    t, d = x2d.shape
    per_b = seq // tm
    ga_blk = col_ga // d
    const = lambda i: (0, 0)
    wsel = lambda i: (layer, 0, 0)
    return pl.pallas_call(
        functools.partial(_merge_kernel, alpha=alpha, tc=min(d, 512)),
        grid=(t // tm,),
        in_specs=[
            pl.BlockSpec((tm, HG_WIDTH), lambda i: (i, 0)),
            pl.BlockSpec((tm, SA_WIDTH), lambda i: (i, 0)),
            pl.BlockSpec((tm, d), lambda i: (i, ga_blk)),
            pl.BlockSpec((tm, d), lambda i: (i, ga_blk + 1)),
            pl.BlockSpec((tm, d), lambda i: (i, 0)),
            pl.BlockSpec((None, 1, d), lambda i: (i // per_b, 0, 0)),
            pl.BlockSpec((None, HG_WIDTH, d), wsel, pipeline_mode=pl.Buffered(1)),
            pl.BlockSpec((None, SA_WIDTH, d), wsel, pipeline_mode=pl.Buffered(1)),
            pl.BlockSpec((None, d, d), wsel, pipeline_mode=pl.Buffered(1)),
            pl.BlockSpec((1, d), const),
            pl.BlockSpec((1, d), const),
        ],
        out_specs=pl.BlockSpec((tm, d), lambda i: (i, 0)),
        out_shape=jax.ShapeDtypeStruct((t, d), F32),
        scratch_shapes=[pltpu.VMEM((tm, d), BF16), pltpu.VMEM((tm, d), F32)],
        compiler_params=_params("arbitrary"),
        name="merge_out",
    )(ya, yb, proj, proj, x2d, gt, wpa, wpb, wout, lng.reshape(1, d), lnb.reshape(1, d))


def _swiglu_kernel(x_ref, sc_ref, sh_ref, gt_ref, wg_ref, wu_ref, wd_ref, lng_ref, lnb_ref,
                   o_ref, hbuf, acc, *, alpha):
    f = pl.program_id(1)

    @pl.when(f == 0)
    def _():
        hbuf[...] = (x_ref[...] * (1.0 + sc_ref[...]) + sh_ref[...]).astype(BF16)
        acc[...] = jnp.zeros_like(acc)

    h = hbuf[...]
    g = _dot(h, wg_ref[...])
    u = _dot(h, wu_ref[...])
    a = (g * _sigmoid(g) * u).astype(BF16)
    acc[...] += _dot(a, wd_ref[...])

    @pl.when(f == pl.num_programs(1) - 1)
    def _():
        z = alpha * x_ref[...] + (1.0 + gt_ref[...]) * acc[...]
        o_ref[...] = _layer_norm(z, lng_ref[...], lnb_ref[...])


def _swiglu(x2d, sc, sh, gt, wg, wu, wd, lng, lnb, layer, seq, alpha, tm, tf):
    t, d = x2d.shape
    dff = wg.shape[2]
    per_b = seq // tm
    mod = pl.BlockSpec((None, 1, d), lambda i, f: (i // per_b, 0, 0))
    const = pl.BlockSpec((1, d), lambda i, f: (0, 0))
    return pl.pallas_call(
        functools.partial(_swiglu_kernel, alpha=alpha),
        grid=(t // tm, dff // tf),
        in_specs=[
            pl.BlockSpec((tm, d), lambda i, f: (i, 0)),
            mod, mod, mod,
            pl.BlockSpec((None, d, tf), lambda i, f: (layer, 0, f)),
            pl.BlockSpec((None, d, tf), lambda i, f: (layer, 0, f)),
            pl.BlockSpec((None, tf, d), lambda i, f: (layer, f, 0)),
            const, const,
        ],
        out_specs=pl.BlockSpec((tm, d), lambda i, f: (i, 0)),
        out_shape=jax.ShapeDtypeStruct((t, d), F32),
        scratch_shapes=[pltpu.VMEM((tm, d), BF16), pltpu.VMEM((tm, d), F32)],
        compiler_params=_params("arbitrary", "arbitrary"),
        name="swiglu",
    )(x2d, sc, sh, gt, wg, wu, wd, lng.reshape(1, d), lnb.reshape(1, d))


def _regroup_kernel(w_ref, o_ref, *, groups):
    row = 0
    for lo, hi in groups:
        o_ref[row:row + hi - lo, :] = w_ref[lo:hi, :].astype(BF16)
        row += hi - lo
    o_ref[row:, :] = jnp.zeros((o_ref.shape[0] - row, o_ref.shape[1]), BF16)


def _regroup_w_in(w_in_t, n_proj):
    depth, d_in, d = w_in_t.shape
    small = SA_Q_RANK + SA_LATENT + IDX_DIM + IDX_HEADS
    hg4 = 4 * HG_WIDTH
    groups = ((0, hg4), (hg4 + small, d_in), (hg4, hg4 + small))
    cols = 256
    return pl.pallas_call(
        functools.partial(_regroup_kernel, groups=groups),
        grid=(depth, d // cols),
        in_specs=[pl.BlockSpec((None, d_in, cols), lambda l, i: (l, 0, i))],
        out_specs=pl.BlockSpec((None, n_proj, cols), lambda l, i: (l, 0, i)),
        out_shape=jax.ShapeDtypeStruct((depth, n_proj, d), BF16),
        compiler_params=_params("arbitrary", "arbitrary"),
        name="w_in_regroup",
    )(w_in_t)


def _largest_tile(n, cap, quantum):
    best = quantum
    for cand in range(quantum, min(n, cap) + 1, quantum):
        if n % cand == 0:
            best = cand
    return best


def kernel(x, c, w_ada, b_ada, w_in, lb_logits, g_hg, w_pa, g_cq, g_ckv, w_uq, w_iq, w_uv, w_pb,
           w_out, ln1_g, ln1_b, w_gate, w_up, w_down, ln2_g, ln2_b):
    batch, seq, d = x.shape
    depth = w_ada.shape[0]
    dff = w_gate.shape[2]
    alpha = (2 * depth) ** 0.25
    assert seq % ATT_TK == 0 and d % 256 == 0 and batch <= ADA_PAD_ROWS

    hg4 = 4 * HG_WIDTH
    col_ga = hg4
    col_cq = hg4 + 2 * d
    used = col_cq + SA_Q_RANK + SA_LATENT + KI_WIDTH
    tn = 1536 if d >= 1024 else 512
    n_proj = -(-used // tn) * tn
    w_in_r = _regroup_w_in(jnp.swapaxes(w_in, 1, 2), n_proj)

    scale = SA_LATENT ** -0.5 * LOG2E
    w_uq_t = _transpose_weights(w_uq, scale)
    w_iq_t = _transpose_weights(w_iq, 1.0)
    w_uvt = jnp.swapaxes(w_uv, 2, 3).astype(BF16)
    w_pa_b, w_pb_b, w_out_b = w_pa.astype(BF16), w_pb.astype(BF16), w_out.astype(BF16)
    w_gate_b, w_up_b, w_down_b = w_gate.astype(BF16), w_up.astype(BF16), w_down.astype(BF16)

    p = jax.nn.softmax(lb_logits.astype(F32), axis=0)
    lbs = jnp.cumsum(p, axis=0) - p[0:1]

    c_pad = jnp.zeros((ADA_PAD_ROWS, d), F32).at[:batch].set(c)
    mods = _ada(c_pad, w_ada, b_ada)[:, :batch].reshape(depth, batch, 6, 1, d)

    tm_proj = _largest_tile(seq, 1024, 256)
    lblk = _largest_tile(seq, 1024, HGRN_SUB)
    tm_prep = ATT_TK
    tm_merge = 256
    tm_ffn = _largest_tile(seq, 512, 256)
    tf = _largest_tile(dff, 512, 128)

    x2d = x.reshape(batch * seq, d)
    for l in range(depth):
        sh_m, sc_m, gt_m, sh_f, sc_f, gt_f = (mods[l, :, k] for k in range(6))
        proj = _inproj(x2d, sc_m, sh_m, w_in_r, l, seq, tm_proj, tn)
        y_a = _hgrn(proj, lbs[l], g_hg[l], batch, seq, lblk)
        qt, qit, kidx, ckv, ckvt, wt = _dsa_prep(proj, g_cq[l], g_ckv[l], w_uq_t, w_iq_t, l, batch, seq, col_cq, tm_prep)
        y_b = _attn(qt, qit, wt, kidx, ckv, ckvt, w_uvt, l, batch, seq)
        x2d = _merge(y_a, y_b, proj, x2d, gt_m, w_pa_b, w_pb_b, w_out_b, ln1_g[l], ln1_b[l],
                     l, seq, col_ga, alpha, tm_merge)
        x2d = _swiglu(x2d, sc_f, sh_f, gt_f, w_gate_b, w_up_b, w_down_b, ln2_g[l], ln2_b[l],
                      l, seq, alpha, tm_ffn, tf)
    return x2d.reshape(batch, seq, d)
```
